```python
import math
import jax, jax.numpy as jnp
from jax import lax
import numpy as np

D_MODEL = 2048
BATCH = 2
SEQ = 8192
DEPTH = 1
DEC_BATCH = 16
DEC_SEQ = 16
PAST_LEN = 4096

CHUNK = 64
FOX_HEADS = 8
FOX_HEAD_DIM = 128
FOX_WIDTH = FOX_HEADS * FOX_HEAD_DIM
FOX_SCALE = FOX_HEAD_DIM ** -0.5
Q_BLOCK = 128
FORGET_BIAS_INIT = 3.0
SGU_GROUPS = 8
SGU_CHUNK = 128
SGU_WIDTH = 1024
SGU_GROUP_DIM = SGU_WIDTH // SGU_GROUPS
PEER_HEADS = 8
PEER_N_KEYS = 128
PEER_N_EXPERTS = PEER_N_KEYS * PEER_N_KEYS
PEER_TOPK = 16
PEER_KEY_DIM = 256
PEER_BLOCK = 128
IN_WIDTH = 3 * FOX_WIDTH + FOX_HEADS + 2 * SGU_WIDTH + 2 * D_MODEL
RMS_EPS = 1e-6
NEG_INF = -1e30

kernel_name = 'fox_sgu_peer_stream_step'


def rms_norm(x, g):
    xf = x.astype(jnp.float32)
    y = xf * lax.rsqrt(jnp.mean(xf * xf, axis=-1, keepdims=True) + RMS_EPS)
    return (y * g.astype(jnp.float32)).astype(x.dtype)


def mixer_inputs(h, w_in, b_forget, q_norm_g, k_norm_g, v_norm_g):
    b, s, _ = h.shape
    z = h @ w_in
    o1 = FOX_WIDTH
    o2 = 2 * FOX_WIDTH
    o3 = 3 * FOX_WIDTH
    o4 = o3 + FOX_HEADS
    o5 = o4 + SGU_WIDTH
    o6 = o5 + SGU_WIDTH
    o7 = o6 + D_MODEL
    q, k, v, f, u_s, v_s, gate_a, gate_b = jnp.split(z, [o1, o2, o3, o4, o5, o6, o7], axis=-1)
    q = rms_norm(q.reshape(b, s, FOX_HEADS, FOX_HEAD_DIM), q_norm_g)
    k = rms_norm(k.reshape(b, s, FOX_HEADS, FOX_HEAD_DIM), k_norm_g)
    v = v.reshape(b, s, FOX_HEADS, FOX_HEAD_DIM)
    logf = jax.nn.log_sigmoid((f + b_forget).astype(jnp.float32))
    u_s = jax.nn.gelu(u_s)
    v_s = rms_norm(jax.nn.gelu(v_s), v_norm_g)
    return q, k, v, logf, u_s, v_s, gate_a, gate_b


def fox_block(q, fq, qpos, k, v, fk, kpos):
    logits = jnp.einsum('bqhd,bkhd->bhqk', q, k).astype(jnp.float32) * FOX_SCALE
    decay = jnp.transpose(fq, (0, 2, 1))[..., :, None] - jnp.transpose(fk, (0, 2, 1))[..., None, :]
    mask = kpos[None, :] <= qpos[:, None]
    logits = jnp.where(mask, logits + decay, NEG_INF)
    p = jax.nn.softmax(logits, axis=-1).astype(v.dtype)
    return jnp.einsum('bhqk,bkhd->bqhd', p, v)


def fox_prompt(q, k, v, logf):
    b, s = q.shape[0], q.shape[1]
    nb = s // Q_BLOCK
    fcum = jnp.cumsum(logf, axis=1)
    pos = jnp.arange(s)
    qb = q.reshape(b, nb, Q_BLOCK, FOX_HEADS, FOX_HEAD_DIM).swapaxes(0, 1)
    fb = fcum.reshape(b, nb, Q_BLOCK, FOX_HEADS).swapaxes(0, 1)
    pb = pos.reshape(nb, Q_BLOCK)
    out = lax.map(lambda a: fox_block(a[0], a[1], a[2], k, v, fcum, pos), (qb, fb, pb))
    return out.swapaxes(0, 1).reshape(b, s, FOX_WIDTH)


def fox_sample(q, k, v, logf, cache_k, cache_v, cache_logf):
    b, t = q.shape[0], q.shape[1]
    p = cache_k.shape[1]
    k_all = jnp.concatenate([cache_k.astype(k.dtype), k], axis=1)
    v_all = jnp.concatenate([cache_v.astype(v.dtype), v], axis=1)
    fcum = jnp.cumsum(jnp.concatenate([cache_logf.astype(jnp.float32), logf], axis=1), axis=1)
    kpos = jnp.arange(p + t)
    qpos = p + jnp.arange(t)
    out = fox_block(q, fcum[:, p:], qpos, k_all, v_all, fcum, kpos)
    return out.reshape(b, t, FOX_WIDTH)


def sgu_prompt(u, v, w_s, b_s):
    b, s, _ = u.shape
    nc = s // SGU_CHUNK
    vb = v.reshape(b, nc, SGU_CHUNK, SGU_GROUPS, SGU_GROUP_DIM)
    w = jnp.tril(w_s)
    mixed = jnp.einsum('gij,bnjgc->bnigc', w, vb) + jnp.transpose(b_s)[:, :, None].astype(v.dtype)
    return u * mixed.reshape(b, s, SGU_WIDTH)


def sgu_sample(u, v, w_s, b_s):
    b, t, _ = u.shape
    vb = v.reshape(b, t, SGU_GROUPS, SGU_GROUP_DIM)
    w = jnp.tril(w_s)[:, :t, :t]
    mixed = jnp.einsum('gij,bjgc->bigc', w, vb) + jnp.transpose(b_s[:, :t])[:, :, None].astype(v.dtype)
    return u * mixed.reshape(b, t, SGU_WIDTH)


def merge_branches(o_a, o_b, gate_a, gate_b, w_out_a, w_out_b, w_out):
    y = jax.nn.sigmoid(gate_a) * (o_a @ w_out_a) + jax.nn.sigmoid(gate_b) * (o_b @ w_out_b)
    return y @ w_out


def peer_ffn(xn, w_q, q_norm_g, sub_keys, table_u, table_v):
    b, s, d = xn.shape
    n = b * s
    xf = xn.reshape(n, d)
    pad = (-n) % PEER_BLOCK
    xf = jnp.pad(xf, ((0, pad), (0, 0)))
    blocks = xf.reshape(-1, PEER_BLOCK, d)

    def one_block(xb):
        q = rms_norm((xb @ w_q).reshape(-1, PEER_HEADS, PEER_KEY_DIM), q_norm_g)
        half = PEER_KEY_DIM // 2
        s1 = jnp.einsum('nhd,hkd->nhk', q[..., :half], sub_keys[:, 0]).astype(jnp.float32)
        s2 = jnp.einsum('nhd,hkd->nhk', q[..., half:], sub_keys[:, 1]).astype(jnp.float32)
        t1, i1 = lax.top_k(s1, PEER_TOPK)
        t2, i2 = lax.top_k(s2, PEER_TOPK)
        cand = (t1[..., :, None] + t2[..., None, :]).reshape(-1, PEER_HEADS, PEER_TOPK * PEER_TOPK)
        cidx = (i1[..., :, None] * PEER_N_KEYS + i2[..., None, :]).reshape(-1, PEER_HEADS, PEER_TOPK * PEER_TOPK)
        st, sel = lax.top_k(cand, PEER_TOPK)
        eidx = jnp.take_along_axis(cidx, sel, axis=-1)
        g = jax.nn.softmax(st, axis=-1)
        u_sel = jnp.take(table_u, eidx, axis=0)
        a = jax.nn.gelu(jnp.einsum('nd,nhkd->nhk', xb, u_sel).astype(jnp.float32))
        v_sel = jnp.take(table_v, eidx, axis=0)
        return jnp.einsum('nhk,nhkd->nd', (g * a).astype(xb.dtype), v_sel)

    y = lax.map(one_block, blocks).reshape(-1, d)[:n]
    return y.reshape(b, s, d)


def setup_inputs(seed: int = 0) -> dict:
    key = jax.random.key(seed)
    ks = jax.random.split(key, 22)
    L = DEPTH

    def nrm(k, shape, scale):
        return jax.random.normal(k, shape, jnp.float32) * scale

    return {
        'x_prompt': nrm(ks[0], (BATCH, SEQ, D_MODEL), 1.0),
        'x_sample': nrm(ks[1], (DEC_BATCH, DEC_SEQ, D_MODEL), 1.0),
        'cache_k': nrm(ks[2], (L, DEC_BATCH, PAST_LEN, FOX_HEADS, FOX_HEAD_DIM), 1.0),
        'cache_v': nrm(ks[3], (L, DEC_BATCH, PAST_LEN, FOX_HEADS, FOX_HEAD_DIM), 1.0),
        'cache_logf': jax.nn.log_sigmoid(FORGET_BIAS_INIT + nrm(ks[4], (L, DEC_BATCH, PAST_LEN, FOX_HEADS), 1.0)),
        'norm_mix_g': 1.0 + nrm(ks[5], (L, D_MODEL), 0.02),
        'w_in': nrm(ks[6], (L, D_MODEL, IN_WIDTH), D_MODEL ** -0.5),
        'b_forget': FORGET_BIAS_INIT + nrm(ks[7], (L, FOX_HEADS), 0.5),
        'q_norm_g': 1.0 + nrm(ks[8], (L, FOX_HEAD_DIM), 0.02),
        'k_norm_g': 1.0 + nrm(ks[9], (L, FOX_HEAD_DIM), 0.02),
        'v_norm_g': 1.0 + nrm(ks[10], (L, SGU_WIDTH), 0.02),
        'w_spatial': nrm(ks[11], (L, SGU_GROUPS, SGU_CHUNK, SGU_CHUNK), SGU_CHUNK ** -0.5),
        'b_spatial': 1.0 + nrm(ks[12], (L, SGU_GROUPS, SGU_CHUNK), 0.1),
        'w_out_a': nrm(ks[13], (L, FOX_WIDTH, D_MODEL), FOX_WIDTH ** -0.5),
        'w_out_b': nrm(ks[14], (L, SGU_WIDTH, D_MODEL), SGU_WIDTH ** -0.5),
        'w_out': nrm(ks[15], (L, D_MODEL, D_MODEL), D_MODEL ** -0.5),
        'norm_ffn_g': 1.0 + nrm(ks[16], (L, D_MODEL), 0.02),
        'w_peer_q': nrm(ks[17], (L, D_MODEL, PEER_HEADS * PEER_KEY_DIM), D_MODEL ** -0.5),
        'peer_q_norm_g': 1.0 + nrm(ks[18], (L, PEER_KEY_DIM), 0.02),
        'peer_sub_keys': nrm(ks[19], (L, PEER_HEADS, 2, PEER_N_KEYS, PEER_KEY_DIM // 2), (PEER_KEY_DIM // 2) ** -0.5),
        'peer_u': nrm(ks[20], (L, PEER_N_EXPERTS, D_MODEL), D_MODEL ** -0.5),
        'peer_v': nrm(ks[21], (L, PEER_N_EXPERTS, D_MODEL), 0.1),
    }


def reference(x_prompt, x_sample, cache_k, cache_v, cache_logf, norm_mix_g, w_in, b_forget,
              q_norm_g, k_norm_g, v_norm_g, w_spatial, b_spatial, w_out_a, w_out_b, w_out,
              norm_ffn_g, w_peer_q, peer_q_norm_g, peer_sub_keys, peer_u, peer_v):
    xp = x_prompt
    xs = x_sample
    kp_l, vp_l, fp_l = [], [], []
    ks_l, vs_l, fs_l, us_l = [], [], [], []
    for l in range(DEPTH):
        h = rms_norm(xp, norm_mix_g[l])
        q, k, v, logf, u_s, v_s, ga, gb = mixer_inputs(h, w_in[l], b_forget[l], q_norm_g[l], k_norm_g[l], v_norm_g[l])
        o_a = fox_prompt(q, k, v, logf)
        o_b = sgu_prompt(u_s, v_s, w_spatial[l], b_spatial[l])
        xp = xp + merge_branches(o_a, o_b, ga, gb, w_out_a[l], w_out_b[l], w_out[l])
        xp = xp + peer_ffn(rms_norm(xp, norm_ffn_g[l]), w_peer_q[l], peer_q_norm_g[l], peer_sub_keys[l], peer_u[l], peer_v[l])
        kp_l.append(k)
        vp_l.append(v)
        fp_l.append(logf)

        h = rms_norm(xs, norm_mix_g[l])
        q, k, v, logf, u_s, v_s, ga, gb = mixer_inputs(h, w_in[l], b_forget[l], q_norm_g[l], k_norm_g[l], v_norm_g[l])
        o_a = fox_sample(q, k, v, logf, cache_k[l], cache_v[l], cache_logf[l])
        o_b = sgu_sample(u_s, v_s, w_spatial[l], b_spatial[l])
        xs = xs + merge_branches(o_a, o_b, ga, gb, w_out_a[l], w_out_b[l], w_out[l])
        xs = xs + peer_ffn(rms_norm(xs, norm_ffn_g[l]), w_peer_q[l], peer_q_norm_g[l], peer_sub_keys[l], peer_u[l], peer_v[l])
        ks_l.append(k)
        vs_l.append(v)
        fs_l.append(logf)
        us_l.append(v_s)
    return (xp, xs, jnp.stack(kp_l), jnp.stack(vp_l), jnp.stack(fp_l),
            jnp.stack(ks_l), jnp.stack(vs_l), jnp.stack(fs_l), jnp.stack(us_l))
```

```python
import functools

import jax
import jax.numpy as jnp
from jax import lax
from jax.experimental import pallas as pl
from jax.experimental.pallas import tpu as pltpu

_F32 = jnp.float32
_BF16 = jnp.bfloat16

_V7X_LANES = 128
_V7X_VMEM_BYTES = 64 * 1024 * 1024
_VMEM_LIMIT = _V7X_VMEM_BYTES - 8 * 1024 * 1024

_RMS_EPS = 1e-6
_NEG = -1e30
_HEADS = 8
_HEAD_DIM = 128
_FOX_SCALE = _HEAD_DIM ** -0.5
_PEER_TOPK = 16
_SGU_CHUNK = 128


def _params(*sem):
    return pltpu.CompilerParams(dimension_semantics=sem, vmem_limit_bytes=_VMEM_LIMIT)


def _rms(x):
    return x * lax.rsqrt(jnp.mean(x * x, axis=-1, keepdims=True) + _RMS_EPS)


def _log_sigmoid(x):
    return jnp.minimum(x, 0.0) - jnp.log1p(jnp.exp(-jnp.abs(x)))


def _dot(a, b):
    return jnp.dot(a, b, preferred_element_type=_F32)


def _dot_nt(a, b):
    return lax.dot_general(a, b, (((1,), (1,)), ((), ())), preferred_element_type=_F32)


def _dot_tn(a, b):
    return lax.dot_general(a, b, (((0,), (0,)), ((), ())), preferred_element_type=_F32)


def _tile(n, pref):
    t = min(n, pref)
    while n % t:
        t //= 2
    return t


def _in_proj_body(x_ref, g_ref, w_ref, wf_ref, bf_ref, qg_ref, kg_ref, vg_ref, ws_ref, bs_ref,
                  q_ref, k_ref, v_ref, lf_ref, ob_ref, ga_ref, gb_ref, *rest, emit_sgu_v):
    if emit_sgu_v:
        sv_ref, hn_scr, u_scr = rest
    else:
        hn_scr, u_scr = rest
    j = pl.program_id(1)
    hd = _HEAD_DIM

    @pl.when(j == 0)
    def _():
        hn = (_rms(x_ref[...]) * g_ref[...]).astype(_BF16)
        hn_scr[...] = hn
        f = _dot(hn, wf_ref[...]) + bf_ref[...]
        lane = lax.broadcasted_iota(jnp.int32, f.shape, 1)
        lf_ref[...] = jnp.where(lane < _HEADS, _log_sigmoid(f), 0.0)

    z = _dot(hn_scr[...], w_ref[...])

    @pl.when(j == 0)
    def _():
        for h in range(_HEADS):
            sl = slice(h * hd, (h + 1) * hd)
            q_ref[:, sl] = (_rms(z[:, sl]) * (qg_ref[...] * _FOX_SCALE)).astype(_BF16)

    @pl.when(j == 1)
    def _():
        for h in range(_HEADS):
            sl = slice(h * hd, (h + 1) * hd)
            k_ref[:, sl] = _rms(z[:, sl]) * kg_ref[...]

    @pl.when(j == 2)
    def _():
        v_ref[...] = z

    @pl.when(j == 3)
    def _():
        u_scr[...] = jax.nn.gelu(z)

    @pl.when(j == 4)
    def _():
        vs = _rms(jax.nn.gelu(z)) * vg_ref[...]
        if emit_sgu_v:
            sv_ref[...] = vs
        c = _SGU_CHUNK
        row = lax.broadcasted_iota(jnp.int32, (c, c), 0)
        col = lax.broadcasted_iota(jnp.int32, (c, c), 1)
        for g in range(_HEADS):
            w = jnp.where(row >= col, ws_ref[g], jnp.zeros((), _BF16))
            for t in range(z.shape[0] // c):
                rs, cs = slice(t * c, (t + 1) * c), slice(g * hd, (g + 1) * hd)
                mixed = _dot(w, vs[rs, cs].astype(_BF16)) + bs_ref[g]
                ob_ref[rs, cs] = (u_scr[rs, cs] * mixed).astype(_BF16)

    @pl.when((j == 5) | (j == 6))
    def _():
        ga_ref[...] = jax.nn.sigmoid(z).astype(_BF16)

    @pl.when(j >= 7)
    def _():
        gb_ref[...] = jax.nn.sigmoid(z).astype(_BF16)


def _in_proj(x, g, w_main, wf, bfp, qg, kg, vg, ws, bs, *, emit_sgu_v):
    n, d = x.shape
    fw = _HEADS * _HEAD_DIM
    tm = _tile(n, 512)
    nblk = w_main.shape[1] // fw
    const = lambda i, j: (0, 0)
    row = lambda i, j: (i, 0)
    out_shape = [
        jax.ShapeDtypeStruct((n, fw), _BF16),
        jax.ShapeDtypeStruct((n, fw), _F32),
        jax.ShapeDtypeStruct((n, fw), _F32),
        jax.ShapeDtypeStruct((n, _V7X_LANES), _F32),
        jax.ShapeDtypeStruct((n, fw), _BF16),
        jax.ShapeDtypeStruct((n, 2 * fw), _BF16),
        jax.ShapeDtypeStruct((n, 2 * fw), _BF16),
    ]
    out_specs = [
        pl.BlockSpec((tm, fw), row), pl.BlockSpec((tm, fw), row), pl.BlockSpec((tm, fw), row),
        pl.BlockSpec((tm, _V7X_LANES), row), pl.BlockSpec((tm, fw), row),
        pl.BlockSpec((tm, fw), lambda i, j: (i, jnp.clip(j - 5, 0, 1))),
        pl.BlockSpec((tm, fw), lambda i, j: (i, jnp.clip(j - 7, 0, 1))),
    ]
    if emit_sgu_v:
        out_shape.append(jax.ShapeDtypeStruct((n, fw), _F32))
        out_specs.append(pl.BlockSpec((tm, fw), row))
    return pl.pallas_call(
        functools.partial(_in_proj_body, emit_sgu_v=emit_sgu_v),
        grid=(n // tm, nblk),
        in_specs=[
            pl.BlockSpec((tm, d), row),
            pl.BlockSpec((1, d), const),
            pl.BlockSpec((d, fw), lambda i, j: (0, j)),
            pl.BlockSpec((d, _V7X_LANES), const),
            pl.BlockSpec((1, _V7X_LANES), const),
            pl.BlockSpec((1, _HEAD_DIM), const),
            pl.BlockSpec((1, _HEAD_DIM), const),
            pl.BlockSpec((1, fw), const),
            pl.BlockSpec(ws.shape, lambda i, j: (0, 0, 0)),
            pl.BlockSpec(bs.shape, lambda i, j: (0, 0, 0)),
        ],
        out_specs=out_specs,
        out_shape=out_shape,
        scratch_shapes=[pltpu.VMEM((tm, d), _BF16), pltpu.VMEM((tm, fw), _F32)],
        compiler_params=_params("arbitrary", "arbitrary"),
        name="in_proj",
    )(x, g, w_main, wf, bfp, qg, kg, vg, ws, bs)


def _split3(x):
    hi = x.astype(_BF16)
    r1 = x - hi.astype(_F32)
    mid = r1.astype(_BF16)
    lo = (r1 - mid.astype(_F32)).astype(_BF16)
    return hi, mid, lo


def _fcum_aug_body(lf_ref, aq_ref, ak_ref, carry_scr):
    @pl.when(pl.program_id(1) == 0)
    def _():
        carry_scr[...] = jnp.zeros_like(carry_scr)

    lf = lf_ref[0]
    n = lf.shape[0]
    tri = (lax.broadcasted_iota(jnp.int32, (n, n), 0) >= lax.broadcasted_iota(jnp.int32, (n, n), 1)).astype(_F32)
    cs = jnp.dot(tri, lf, precision=lax.Precision.HIGHEST, preferred_element_type=_F32) + carry_scr[0:1, :]
    carry_scr[...] = jnp.broadcast_to(cs[n - 1:n, :], carry_scr.shape)
    pieces = _split3(cs)

    fw = _HEADS * _HEAD_DIM
    src = lax.broadcasted_iota(jnp.int32, (_V7X_LANES, fw), 0)
    dst = lax.broadcasted_iota(jnp.int32, (_V7X_LANES, fw), 1)

    def place(p):
        return jnp.where(dst == src * _HEAD_DIM + p, 1.0, 0.0).astype(_BF16)

    fq = _dot(pieces[0], place(0)) + _dot(pieces[1], place(1)) + _dot(pieces[2], place(2))
    fk = _dot(pieces[0], place(3)) + _dot(pieces[1], place(4)) + _dot(pieces[2], place(5))
    lane = lax.broadcasted_iota(jnp.int32, (1, fw), 1) % _HEAD_DIM
    aq_ref[0] = (fq + jnp.where((lane >= 3) & (lane < 6), 1.0, 0.0)).astype(_BF16)
    ak_ref[0] = (jnp.where(lane < 3, 1.0, 0.0) - fk).astype(_BF16)


def _fcum_aug(lf):
    b, s, lanes = lf.shape
    fw = _HEADS * _HEAD_DIM
    tl = _tile(s, 512)
    blk = lambda i, c: (i, c, 0)
    return pl.pallas_call(
        _fcum_aug_body,
        grid=(b, s // tl),
        in_specs=[pl.BlockSpec((1, tl, lanes), blk)],
        out_specs=[pl.BlockSpec((1, tl, fw), blk), pl.BlockSpec((1, tl, fw), blk)],
        out_shape=[jax.ShapeDtypeStruct((b, s, fw), _BF16)] * 2,
        scratch_shapes=[pltpu.VMEM((8, lanes), _F32)],
        compiler_params=_params("arbitrary", "arbitrary"),
        name="fcum_aug",
    )(lf)


def _fox_prompt_body(q_ref, aq_ref, k_ref, ak_ref, v_ref, o_ref, kc_scr, vc_scr, m_scr, l_scr, acc_scr, *, tq):
    qi = pl.program_id(2)
    hd = _HEAD_DIM

    @pl.when(qi == 0)
    def _():
        kc_scr[:, :hd] = k_ref[0].astype(_BF16)
        kc_scr[:, hd:] = ak_ref[0]
        vc_scr[...] = v_ref[0].astype(_BF16)

    qc = jnp.concatenate([q_ref[0], aq_ref[0]], axis=1)
    m_scr[...] = jnp.full_like(m_scr, _NEG)
    l_scr[...] = jnp.zeros_like(l_scr)
    acc_scr[...] = jnp.zeros_like(acc_scr)

    def tile(kj, masked):
        off = pl.multiple_of(kj * tq, tq)
        s = _dot_nt(qc, kc_scr[pl.ds(off, tq), :])
        if masked:
            row = lax.broadcasted_iota(jnp.int32, s.shape, 0)
            col = lax.broadcasted_iota(jnp.int32, s.shape, 1)
            s = jnp.where(col <= row, s, _NEG)
        m_old = m_scr[...]
        m_new = jnp.maximum(m_old, jnp.max(s, axis=-1, keepdims=True))
        alpha = jnp.exp(m_old - m_new)
        p = jnp.exp(s - m_new)
        l_scr[...] = alpha * l_scr[...] + jnp.sum(p, axis=-1, keepdims=True)
        acc_scr[...] = alpha * acc_scr[...] + _dot(p.astype(_BF16), vc_scr[pl.ds(off, tq), :])
        m_scr[...] = m_new

    def body(kj, c):
        tile(kj, False)
        return c

    lax.fori_loop(0, qi, body, 0)
    tile(qi, True)
    o_ref[0] = (acc_scr[...] / l_scr[...]).astype(o_ref.dtype)


def _fox_prompt(q, aq, k, ak, v):
    b, s, fw = q.shape
    hd = _HEAD_DIM
    tq = _tile(s, 512)
    qblk = lambda i, h, t: (i, t, h)
    kblk = lambda i, h, t: (i, 0, h)
    return pl.pallas_call(
        functools.partial(_fox_prompt_body, tq=tq),
        grid=(b, _HEADS, s // tq),
        in_specs=[
            pl.BlockSpec((1, tq, hd), qblk), pl.BlockSpec((1, tq, hd), qblk),
            pl.BlockSpec((1, s, hd), kblk), pl.BlockSpec((1, s, hd), kblk), pl.BlockSpec((1, s, hd), kblk),
        ],
        out_specs=pl.BlockSpec((1, tq, hd), qblk),
        out_shape=jax.ShapeDtypeStruct((b, s, fw), _BF16),
        scratch_shapes=[
            pltpu.VMEM((s, 2 * hd), _BF16), pltpu.VMEM((s, hd), _BF16),
            pltpu.VMEM((tq, 1), _F32), pltpu.VMEM((tq, 1), _F32), pltpu.VMEM((tq, hd), _F32),
        ],
        compiler_params=_params("arbitrary", "arbitrary", "arbitrary"),
        name="fox_prompt",
    )(q, aq, k, ak, v)


def _fox_decode_body(q_ref, kn_ref, vn_ref, lf_ref, lft_ref, ck_ref, cv_ref, clft_ref, o_ref,
                     carry_scr, m_scr, l_scr, acc_scr):
    kt = pl.program_id(1)
    nkt = pl.num_programs(1)
    hd = _HEAD_DIM
    t_new = q_ref.shape[1]
    tk = ck_ref.shape[1]

    lf = lf_ref[0]
    step = lax.broadcasted_iota(jnp.int32, lf.shape, 0)
    e_col = jnp.zeros_like(lf)
    for i in range(t_new):
        e_col = e_col + jnp.where(step >= i, lf[i:i + 1, :], 0.0)

    @pl.when(kt == 0)
    def _():
        carry_scr[...] = jnp.zeros_like(carry_scr)
        m_scr[...] = jnp.full_like(m_scr, _NEG)
        l_scr[...] = jnp.zeros_like(l_scr)
        acc_scr[...] = jnp.zeros_like(acc_scr)

    def update(h, s, vt):
        m_old = m_scr[h]
        m_new = jnp.maximum(m_old, jnp.max(s, axis=-1, keepdims=True))
        alpha = jnp.exp(m_old - m_new)
        p = jnp.exp(s - m_new)
        l_scr[h] = alpha * l_scr[h] + jnp.sum(p, axis=-1, keepdims=True)
        acc_scr[h] = alpha * acc_scr[h] + _dot(p.astype(_BF16), vt)
        m_scr[h] = m_new

    clft = clft_ref[0]
    after = (lax.broadcasted_iota(jnp.int32, (tk, tk), 0) > lax.broadcasted_iota(jnp.int32, (tk, tk), 1)).astype(_F32)
    d_row = jnp.dot(clft, after, precision=lax.Precision.HIGHEST, preferred_element_type=_F32) + carry_scr[:, 0:1]
    carry_scr[...] = carry_scr[...] + jnp.sum(clft, axis=-1, keepdims=True)
    for h in range(_HEADS):
        sl = slice(h * hd, (h + 1) * hd)
        s = _dot_nt(q_ref[0, :, sl], ck_ref[0, :, sl].astype(_BF16)) + e_col[:, h:h + 1] + d_row[h:h + 1, :]
        update(h, s, cv_ref[0, :, sl].astype(_BF16))

    @pl.when(kt == nkt - 1)
    def _():
        lft = lft_ref[0]
        pos = lax.broadcasted_iota(jnp.int32, lft.shape, 1)
        e_row = jnp.zeros_like(lft)
        for i in range(t_new):
            e_row = e_row + jnp.where(pos >= i, lft[:, i:i + 1], 0.0)
        row = lax.broadcasted_iota(jnp.int32, (t_new, t_new), 0)
        col = lax.broadcasted_iota(jnp.int32, (t_new, t_new), 1)
        for h in range(_HEADS):
            sl = slice(h * hd, (h + 1) * hd)
            s = _dot_nt(q_ref[0, :, sl], kn_ref[0, :, sl].astype(_BF16)) + e_col[:, h:h + 1] - e_row[h:h + 1, :]
            update(h, jnp.where(col <= row, s, _NEG), vn_ref[0, :, sl].astype(_BF16))
            o_ref[0, :, sl] = (acc_scr[h] / l_scr[h]).astype(o_ref.dtype)


def _fox_decode(q, kn, vn, lf, lft, ck, cv, clft):
    b, t, fw = q.shape
    p = ck.shape[1]
    tk = _tile(p, 512)
    nkt = p // tk
    new = lambda i, c: (i, 0, 0)
    return pl.pallas_call(
        _fox_decode_body,
        grid=(b, nkt),
        in_specs=[
            pl.BlockSpec((1, t, fw), new), pl.BlockSpec((1, t, fw), new), pl.BlockSpec((1, t, fw), new),
            pl.BlockSpec((1, t, lf.shape[2]), new), pl.BlockSpec((1, _HEADS, t), new),
            pl.BlockSpec((1, tk, fw), lambda i, c: (i, nkt - 1 - c, 0)),
            pl.BlockSpec((1, tk, fw), lambda i, c: (i, nkt - 1 - c, 0)),
            pl.BlockSpec((1, _HEADS, tk), lambda i, c: (i, 0, nkt - 1 - c)),
        ],
        out_specs=pl.BlockSpec((1, t, fw), new),
        out_shape=jax.ShapeDtypeStruct((b, t, fw), _BF16),
        scratch_shapes=[
            pltpu.VMEM((_HEADS, _V7X_LANES), _F32),
            pltpu.VMEM((_HEADS, t, 1), _F32), pltpu.VMEM((_HEADS, t, 1), _F32),
            pltpu.VMEM((_HEADS, t, _HEAD_DIM), _F32),
        ],
        compiler_params=_params("arbitrary", "arbitrary"),
        name="fox_decode",
    )(q, kn, vn, lf, lft, ck, cv, clft)


def _merge_body(x_ref, oa_ref, ob_ref, ga_ref, gb_ref, wa_ref, wb_ref, wo_ref, g2_ref, x1_ref, xn_ref):
    y = ga_ref[...].astype(_F32) * _dot(oa_ref[...], wa_ref[...]) + gb_ref[...].astype(_F32) * _dot(ob_ref[...], wb_ref[...])
    x1 = x_ref[...] + _dot(y.astype(_BF16), wo_ref[...])
    x1_ref[...] = x1
    xn_ref[...] = (_rms(x1) * g2_ref[...]).astype(_BF16)


def _merge(x, oa, ob, ga, gb, wa, wb, wo, g2):
    n, d = x.shape
    fw = oa.shape[1]
    tm = _tile(n, 256)
    row = lambda i: (i, 0)
    const = lambda i: (0, 0)
    once = pl.Buffered(1)
    return pl.pallas_call(
        _merge_body,
        grid=(n // tm,),
        in_specs=[
            pl.BlockSpec((tm, d), row), pl.BlockSpec((tm, fw), row), pl.BlockSpec((tm, fw), row),
            pl.BlockSpec((tm, d), row), pl.BlockSpec((tm, d), row),
            pl.BlockSpec((fw, d), const, pipeline_mode=once), pl.BlockSpec((fw, d), const, pipeline_mode=once),
            pl.BlockSpec((d, d), const, pipeline_mode=once), pl.BlockSpec((1, d), const),
        ],
        out_specs=[pl.BlockSpec((tm, d), row), pl.BlockSpec((tm, d), row)],
        out_shape=[jax.ShapeDtypeStruct((n, d), _F32), jax.ShapeDtypeStruct((n, d), _BF16)],
        compiler_params=_params("arbitrary"),
        name="merge",
    )(x, oa, ob, ga, gb, wa, wb, wo, g2)


def _peer_scores_body(xn_ref, wq_ref, pg_ref, keys_ref, s1_ref, s2_ref):
    pq = _dot(xn_ref[...], wq_ref[...])
    kd = pg_ref.shape[1]
    half = kd // 2
    for h in range(_HEADS):
        qn = (_rms(pq[:, h * kd:(h + 1) * kd]) * pg_ref[...]).astype(_BF16)
        s1_ref[h] = _dot_nt(keys_ref[h, 0], qn[:, :half])
        s2_ref[h] = _dot_nt(keys_ref[h, 1], qn[:, half:])


def _peer_scores(xn, wq, pg, keys):
    n, d = xn.shape
    nk = keys.shape[2]
    tm = _tile(n, 512)
    return pl.pallas_call(
        _peer_scores_body,
        grid=(n // tm,),
        in_specs=[
            pl.BlockSpec((tm, d), lambda i: (i, 0)),
            pl.BlockSpec(wq.shape, lambda i: (0, 0), pipeline_mode=pl.Buffered(1)),
            pl.BlockSpec(pg.shape, lambda i: (0, 0)),
            pl.BlockSpec(keys.shape, lambda i: (0, 0, 0, 0)),
        ],
        out_specs=[pl.BlockSpec((_HEADS, nk, tm), lambda i: (0, 0, i))] * 2,
        out_shape=[jax.ShapeDtypeStruct((_HEADS, nk, n), _F32)] * 2,
        compiler_params=_params("arbitrary"),
        name="peer_scores",
    )(xn, wq, pg, keys)


def _top_rows(a, k):
    out = jnp.full((k, a.shape[1]), _NEG, _F32)
    slot = lax.broadcasted_iota(jnp.int32, out.shape, 0)
    for r in range(k):
        m = jnp.max(a, axis=0, keepdims=True)
        out = jnp.where(slot == r, m, out)
        a = jnp.where(a == m, _NEG, a)
    return out


def _peer_topk_body(s1_ref, s2_ref, tau_ref, cz_ref):
    k = _PEER_TOPK
    for h in range(_HEADS):
        t1 = _top_rows(s1_ref[h], k)
        t2 = _top_rows(s2_ref[h], k)
        cand = jnp.concatenate([t2 + t1[a:a + 1, :] for a in range(k)], axis=0)
        top = _top_rows(cand, k)
        z = jnp.sum(jnp.exp(top - top[0:1, :]), axis=0, keepdims=True)
        tau_ref[h:h + 1, :] = top[k - 1:k, :]
        cz_ref[h:h + 1, :] = top[0:1, :] + jnp.log(z)


def _peer_topk(s1, s2):
    heads, nk, n = s1.shape
    tl = _tile(n, 256)
    sblk = pl.BlockSpec((heads, nk, tl), lambda i: (0, 0, i))
    oblk = pl.BlockSpec((heads, tl), lambda i: (0, i))
    return pl.pallas_call(
        _peer_topk_body,
        grid=(n // tl,),
        in_specs=[sblk, sblk],
        out_specs=[oblk, oblk],
        out_shape=[jax.ShapeDtypeStruct((heads, n), _F32)] * 2,
        compiler_params=_params("arbitrary"),
        name="peer_topk",
    )(s1, s2)


def _peer_dense_body(x1_ref, xn_ref, s1_ref, s2_ref, tau_ref, cz_ref, u_ref, v_ref, o_ref, a_scr, wa_scr, *, tl):
    j = pl.program_id(1)
    te, tm = a_scr.shape
    nk = s2_ref.shape[1]
    groups = te // nk

    @pl.when(j == 0)
    def _():
        o_ref[...] = x1_ref[...]

    a_scr[...] = jax.nn.gelu(_dot_nt(u_ref[...], xn_ref[...]))

    def lanes(c, carry):
        ls = pl.ds(pl.multiple_of(c * tl, tl), tl)
        for r in range(groups):
            w = jnp.zeros((nk, tl), _F32)
            for h in range(_HEADS):
                ssum = s2_ref[h, :, ls] + s1_ref[h, r:r + 1, ls]
                gate = jnp.exp(ssum - cz_ref[h:h + 1, ls])
                w = w + jnp.where(ssum >= tau_ref[h:h + 1, ls], gate, 0.0)
            rows = slice(r * nk, (r + 1) * nk)
            wa_scr[rows, ls] = (w * a_scr[rows, ls]).astype(_BF16)
        return carry

    lax.fori_loop(0, tm // tl, lanes, 0)
    o_ref[...] += _dot_tn(wa_scr[...], v_ref[...])


def _peer_dense(x1, xn, s1, s2, tau, cz, tu, tv):
    n, d = x1.shape
    heads, nk, _ = s1.shape
    ne = tu.shape[0]
    tm = _tile(n, 512)
    groups = 8
    te = groups * nk
    tok = lambda t, j: (t, 0)
    sblk = pl.BlockSpec((heads, nk, tm), lambda t, j: (0, 0, t))
    s1blk = pl.BlockSpec((heads, groups, tm), lambda t, j: (0, j, t))
    hblk = pl.BlockSpec((heads, tm), lambda t, j: (0, t))
    return pl.pallas_call(
        functools.partial(_peer_dense_body, tl=_V7X_LANES),
        grid=(n // tm, ne // te),
        in_specs=[
            pl.BlockSpec((tm, d), tok), pl.BlockSpec((tm, d), tok), s1blk, sblk, hblk, hblk,
            pl.BlockSpec((te, d), lambda t, j: (j, 0)), pl.BlockSpec((te, d), lambda t, j: (j, 0)),
        ],
        out_specs=pl.BlockSpec((tm, d), tok),
        out_shape=jax.ShapeDtypeStruct((n, d), _F32),
        scratch_shapes=[pltpu.VMEM((te, tm), _F32), pltpu.VMEM((te, tm), _BF16)],
        compiler_params=_params("arbitrary", "arbitrary"),
        name="peer_dense",
    )(x1, xn, s1, s2, tau, cz, tu, tv)


def _prep_layer(l, norm_mix_g, w_in, b_forget, q_norm_g, k_norm_g, v_norm_g, w_spatial, b_spatial,
                w_out_a, w_out_b, w_out, norm_ffn_g, w_peer_q, peer_q_norm_g, peer_sub_keys, peer_u, peer_v):
    fw = _HEADS * _HEAD_DIM
    w = w_in[l]
    o3 = 3 * fw
    o4 = o3 + _HEADS
    w_main = jnp.concatenate([w[:, :o3], w[:, o4:]], axis=1).astype(_BF16)
    wf = jnp.pad(w[:, o3:o4], ((0, 0), (0, _V7X_LANES - _HEADS))).astype(_BF16)
    bfp = jnp.pad(b_forget[l], (0, _V7X_LANES - _HEADS)).reshape(1, _V7X_LANES)
    return dict(
        g1=norm_mix_g[l].reshape(1, -1), w_main=w_main, wf=wf, bfp=bfp,
        qg=q_norm_g[l].reshape(1, -1), kg=k_norm_g[l].reshape(1, -1), vg=v_norm_g[l].reshape(1, -1),
        w_spatial=w_spatial[l], b_spatial=b_spatial[l],
        wa=w_out_a[l].astype(_BF16), wb=w_out_b[l].astype(_BF16), wo=w_out[l].astype(_BF16),
        g2=norm_ffn_g[l].reshape(1, -1), wq=w_peer_q[l].astype(_BF16), pg=peer_q_norm_g[l].reshape(1, -1),
        keys=peer_sub_keys[l].astype(_BF16), tu=peer_u[l].astype(_BF16), tv=peer_v[l].astype(_BF16),
    )


def _sgu_weights(p, frames):
    reps = _SGU_CHUNK // frames
    w = p["w_spatial"][:, :frames, :frames]
    eye = jnp.eye(reps, dtype=w.dtype)
    ws = jnp.einsum("ab,gij->gaibj", eye, w).reshape(_HEADS, _SGU_CHUNK, _SGU_CHUNK).astype(_BF16)
    bs = jnp.tile(p["b_spatial"][:, :frames], (1, reps))
    bs = jnp.broadcast_to(bs[:, :, None], (_HEADS, _SGU_CHUNK, _HEAD_DIM))
    return ws, bs


def _peer(p, x1, xn):
    s1, s2 = _peer_scores(xn, p["wq"], p["pg"], p["keys"])
    tau, cz = _peer_topk(s1, s2)
    return _peer_dense(x1, xn, s1, s2, tau, cz, p["tu"], p["tv"])


def _layer_prompt(p, xp):
    b, s, d = xp.shape
    fw = _HEADS * _HEAD_DIM
    x = xp.reshape(b * s, d)
    ws, bs = _sgu_weights(p, _SGU_CHUNK)
    q, k, v, lf, ob, ga, gb = _in_proj(x, p["g1"], p["w_main"], p["wf"], p["bfp"], p["qg"], p["kg"], p["vg"],
                                       ws, bs, emit_sgu_v=False)
    aq, ak = _fcum_aug(lf.reshape(b, s, -1))
    oa = _fox_prompt(q.reshape(b, s, fw), aq, k.reshape(b, s, fw), ak, v.reshape(b, s, fw))
    x1, xn = _merge(x, oa.reshape(b * s, fw), ob, ga, gb, p["wa"], p["wb"], p["wo"], p["g2"])
    y = _peer(p, x1, xn)
    return (y.reshape(b, s, d), k.reshape(b, s, _HEADS, _HEAD_DIM), v.reshape(b, s, _HEADS, _HEAD_DIM),
            lf[:, :_HEADS].reshape(b, s, _HEADS))


def _layer_sample(p, xs, ck, cv, clf):
    b, t, d = xs.shape
    fw = _HEADS * _HEAD_DIM
    past = ck.shape[1]
    x = xs.reshape(b * t, d)
    ws, bs = _sgu_weights(p, t)
    q, k, v, lf, ob, ga, gb, sv = _in_proj(x, p["g1"], p["w_main"], p["wf"], p["bfp"], p["qg"], p["kg"], p["vg"],
                                           ws, bs, emit_sgu_v=True)
    lf3 = lf.reshape(b, t, -1)
    lft = jnp.swapaxes(lf3[:, :, :_HEADS], 1, 2)
    clft = jnp.swapaxes(clf.astype(_F32), 1, 2)
    oa = _fox_decode(q.reshape(b, t, fw), k.reshape(b, t, fw), v.reshape(b, t, fw), lf3, lft,
                     ck.reshape(b, past, fw), cv.reshape(b, past, fw), clft)
    x1, xn = _merge(x, oa.reshape(b * t, fw), ob, ga, gb, p["wa"], p["wb"], p["wo"], p["g2"])
    y = _peer(p, x1, xn)
    return (y.reshape(b, t, d), k.reshape(b, t, _HEADS, _HEAD_DIM), v.reshape(b, t, _HEADS, _HEAD_DIM),
            lf[:, :_HEADS].reshape(b, t, _HEADS), sv.reshape(b, t, fw))


def kernel(x_prompt, x_sample, cache_k, cache_v, cache_logf, norm_mix_g, w_in, b_forget, q_norm_g, k_norm_g, v_norm_g, w_spatial, b_spatial, w_out_a, w_out_b, w_out, norm_ffn_g, w_peer_q, peer_q_norm_g, peer_sub_keys, peer_u, peer_v):
    xp, xs = x_prompt, x_sample
    kp_l, vp_l, fp_l, ks_l, vs_l, fs_l, us_l = [], [], [], [], [], [], []
    for l in range(w_in.shape[0]):
        p = _prep_layer(l, norm_mix_g, w_in, b_forget, q_norm_g, k_norm_g, v_norm_g, w_spatial, b_spatial,
                        w_out_a, w_out_b, w_out, norm_ffn_g, w_peer_q, peer_q_norm_g, peer_sub_keys, peer_u, peer_v)
        xp, k, v, f = _layer_prompt(p, xp)
        kp_l.append(k)
        vp_l.append(v)
        fp_l.append(f)
        xs, k, v, f, u = _layer_sample(p, xs, cache_k[l], cache_v[l], cache_logf[l])
        ks_l.append(k)
        vs_l.append(v)
        fs_l.append(f)
        us_l.append(u)
    return (xp, xs, jnp.stack(kp_l), jnp.stack(vp_l), jnp.stack(fp_l),
            jnp.stack(ks_l), jnp.stack(vs_l), jnp.stack(fs_l), jnp.stack(us_l))
```

```python
import functools

import jax
import jax.numpy as jnp
from jax import lax
from jax.experimental import pallas as pl
from jax.experimental.pallas import tpu as pltpu

_F32 = jnp.float32
_BF16 = jnp.bfloat16

_V7X_LANES = 128
_V7X_VMEM_BYTES = 64 * 1024 * 1024
_VMEM_LIMIT = _V7X_VMEM_BYTES - 8 * 1024 * 1024

_RMS_EPS = 1e-6
_NEG = -1e30
_HEADS = 8
_HEAD_DIM = 128
_FOX_SCALE = _HEAD_DIM ** -0.5
_LOG2E = 1.4426950408889634
_PEER_TOPK = 16
_SGU_CHUNK = 128


def _params(*sem):
    return pltpu.CompilerParams(dimension_semantics=sem, vmem_limit_bytes=_VMEM_LIMIT)


def _rms(x):
    return x * lax.rsqrt(jnp.mean(x * x, axis=-1, keepdims=True) + _RMS_EPS)


def _log_sigmoid(x):
    return jnp.minimum(x, 0.0) - jnp.log1p(jnp.exp(-jnp.abs(x)))


def _dot(a, b):
    return jnp.dot(a, b, preferred_element_type=_F32)


def _dot_nt(a, b):
    return lax.dot_general(a, b, (((1,), (1,)), ((), ())), preferred_element_type=_F32)


def _tile(n, pref):
    t = min(n, pref)
    while n % t:
        t //= 2
    return t


def _in_proj_body(x_ref, g_ref, w_ref, wf_ref, bf_ref, qg_ref, kg_ref, vg_ref, ws_ref, bs_ref,
                  q_ref, k_ref, v_ref, lf_ref, ob_ref, ga_ref, gb_ref, extra_ref, hn_scr, u_scr, *, prompt):
    j = pl.program_id(1)
    hd = _HEAD_DIM

    @pl.when(j == 0)
    def _():
        hn = (_rms(x_ref[...]) * g_ref[...]).astype(_BF16)
        hn_scr[...] = hn
        f = _dot(hn, wf_ref[...]) + bf_ref[...]
        lane = lax.broadcasted_iota(jnp.int32, f.shape, 1)
        lf_ref[...] = jnp.where(lane < _HEADS, _log_sigmoid(f), 0.0)

    z = _dot(hn_scr[...], w_ref[...])

    @pl.when(j == 0)
    def _():
        for h in range(_HEADS):
            sl = slice(h * hd, (h + 1) * hd)
            q_ref[:, sl] = (_rms(z[:, sl]) * (qg_ref[...] * (_FOX_SCALE * _LOG2E))).astype(_BF16)

    @pl.when(j == 1)
    def _():
        for h in range(_HEADS):
            sl = slice(h * hd, (h + 1) * hd)
            k_ref[:, sl] = _rms(z[:, sl]) * kg_ref[...]

    @pl.when(j == 2)
    def _():
        v_ref[...] = z
        if prompt:
            extra_ref[...] = z.T.astype(_BF16)

    @pl.when(j == 3)
    def _():
        u_scr[...] = jax.nn.gelu(z)

    @pl.when(j == 4)
    def _():
        vs = _rms(jax.nn.gelu(z)) * vg_ref[...]
        if not prompt:
            extra_ref[...] = vs
        c = _SGU_CHUNK
        row = lax.broadcasted_iota(jnp.int32, (c, c), 0)
        col = lax.broadcasted_iota(jnp.int32, (c, c), 1)
        for g in range(_HEADS):
            w = jnp.where(row >= col, ws_ref[g], jnp.zeros((), _BF16))
            for t in range(z.shape[0] // c):
                rs, cs = slice(t * c, (t + 1) * c), slice(g * hd, (g + 1) * hd)
                mixed = _dot(w, vs[rs, cs].astype(_BF16)) + bs_ref[g]
                ob_ref[rs, cs] = (u_scr[rs, cs] * mixed).astype(_BF16)

    @pl.when((j == 5) | (j == 6))
    def _():
        ga_ref[...] = jax.nn.sigmoid(z).astype(_BF16)

    @pl.when(j >= 7)
    def _():
        gb_ref[...] = jax.nn.sigmoid(z).astype(_BF16)


def _in_proj(x, g, w_main, wf, bfp, qg, kg, vg, ws, bs, *, prompt):
    n, d = x.shape
    fw = _HEADS * _HEAD_DIM
    tm = _tile(n, 512)
    nblk = w_main.shape[1] // fw
    const = lambda i, j: (0, 0)
    row = lambda i, j: (i, 0)
    out_shape = [
        jax.ShapeDtypeStruct((n, fw), _BF16),
        jax.ShapeDtypeStruct((n, fw), _F32),
        jax.ShapeDtypeStruct((n, fw), _F32),
        jax.ShapeDtypeStruct((n, _V7X_LANES), _F32),
        jax.ShapeDtypeStruct((n, fw), _BF16),
        jax.ShapeDtypeStruct((n, 2 * fw), _BF16),
        jax.ShapeDtypeStruct((n, 2 * fw), _BF16),
    ]
    out_specs = [
        pl.BlockSpec((tm, fw), row), pl.BlockSpec((tm, fw), row), pl.BlockSpec((tm, fw), row),
        pl.BlockSpec((tm, _V7X_LANES), row), pl.BlockSpec((tm, fw), row),
        pl.BlockSpec((tm, fw), lambda i, j: (i, jnp.clip(j - 5, 0, 1))),
        pl.BlockSpec((tm, fw), lambda i, j: (i, jnp.clip(j - 7, 0, 1))),
    ]
    if prompt:
        out_shape.append(jax.ShapeDtypeStruct((fw, n), _BF16))
        out_specs.append(pl.BlockSpec((fw, tm), lambda i, j: (0, i)))
    else:
        out_shape.append(jax.ShapeDtypeStruct((n, fw), _F32))
        out_specs.append(pl.BlockSpec((tm, fw), row))
    return pl.pallas_call(
        functools.partial(_in_proj_body, prompt=prompt),
        grid=(n // tm, nblk),
        in_specs=[
            pl.BlockSpec((tm, d), row),
            pl.BlockSpec((1, d), const),
            pl.BlockSpec((d, fw), lambda i, j: (0, j)),
            pl.BlockSpec((d, _V7X_LANES), const),
            pl.BlockSpec((1, _V7X_LANES), const),
            pl.BlockSpec((1, _HEAD_DIM), const),
            pl.BlockSpec((1, _HEAD_DIM), const),
            pl.BlockSpec((1, fw), const),
            pl.BlockSpec(ws.shape, lambda i, j: (0, 0, 0)),
            pl.BlockSpec(bs.shape, lambda i, j: (0, 0, 0)),
        ],
        out_specs=out_specs,
        out_shape=out_shape,
        scratch_shapes=[pltpu.VMEM((tm, d), _BF16), pltpu.VMEM((tm, fw), _F32)],
        compiler_params=_params("arbitrary", "arbitrary"),
        name="in_proj",
    )(x, g, w_main, wf, bfp, qg, kg, vg, ws, bs)


def _split3(x):
    hi = x.astype(_BF16)
    r1 = x - hi.astype(_F32)
    mid = r1.astype(_BF16)
    lo = (r1 - mid.astype(_F32)).astype(_BF16)
    return hi, mid, lo


def _fcum_aug_body(lf_ref, aq_ref, ak_ref, carry_scr):
    @pl.when(pl.program_id(1) == 0)
    def _():
        carry_scr[...] = jnp.zeros_like(carry_scr)

    lf = lf_ref[0]
    n = lf.shape[0]
    tri = (lax.broadcasted_iota(jnp.int32, (n, n), 0) >= lax.broadcasted_iota(jnp.int32, (n, n), 1)).astype(_F32)
    cs = jnp.dot(tri, lf, precision=lax.Precision.HIGHEST, preferred_element_type=_F32) + carry_scr[0:1, :]
    carry_scr[...] = jnp.broadcast_to(cs[n - 1:n, :], carry_scr.shape)
    pieces = _split3(cs * _LOG2E)

    fw = _HEADS * _HEAD_DIM
    src = lax.broadcasted_iota(jnp.int32, (_V7X_LANES, fw), 0)
    dst = lax.broadcasted_iota(jnp.int32, (_V7X_LANES, fw), 1)

    def place(p):
        return jnp.where(dst == src * _HEAD_DIM + p, 1.0, 0.0).astype(_BF16)

    fq = _dot(pieces[0], place(0)) + _dot(pieces[1], place(1)) + _dot(pieces[2], place(2))
    fk = _dot(pieces[0], place(3)) + _dot(pieces[1], place(4)) + _dot(pieces[2], place(5))
    lane = lax.broadcasted_iota(jnp.int32, (1, fw), 1) % _HEAD_DIM
    aq_ref[0] = (fq + jnp.where((lane >= 3) & (lane < 6), 1.0, 0.0)).astype(_BF16)
    ak_ref[0] = (jnp.where(lane < 3, 1.0, 0.0) - fk).astype(_BF16)


def _fcum_aug(lf):
    b, s, lanes = lf.shape
    fw = _HEADS * _HEAD_DIM
    tl = _tile(s, 512)
    blk = lambda i, c: (i, c, 0)
    return pl.pallas_call(
        _fcum_aug_body,
        grid=(b, s // tl),
        in_specs=[pl.BlockSpec((1, tl, lanes), blk)],
        out_specs=[pl.BlockSpec((1, tl, fw), blk), pl.BlockSpec((1, tl, fw), blk)],
        out_shape=[jax.ShapeDtypeStruct((b, s, fw), _BF16)] * 2,
        scratch_shapes=[pltpu.VMEM((8, lanes), _F32)],
        compiler_params=_params("arbitrary", "arbitrary"),
        name="fcum_aug",
    )(lf)


def _fox_prompt_body(q_ref, aq_ref, k_ref, ak_ref, vt_ref, o_ref, kc_scr, m_scr, l_scr, acc_scr, *, tq):
    qi = pl.program_id(2)
    hd = _HEAD_DIM

    @pl.when(qi == 0)
    def _():
        kc_scr[:, :hd] = k_ref[0].astype(_BF16)
        kc_scr[:, hd:] = ak_ref[0]

    qt = jnp.concatenate([q_ref[0].astype(_F32).T, aq_ref[0].astype(_F32).T], axis=0).astype(_BF16)
    m_scr[...] = jnp.full_like(m_scr, _NEG)
    l_scr[...] = jnp.zeros_like(l_scr)
    acc_scr[...] = jnp.zeros_like(acc_scr)

    def tile(kj, diag):
        off = pl.multiple_of(kj * tq, tq)
        st = _dot(kc_scr[pl.ds(off, tq), :], qt)
        if diag:
            key = lax.broadcasted_iota(jnp.int32, st.shape, 0)
            qry = lax.broadcasted_iota(jnp.int32, st.shape, 1)
            st = jnp.where(key <= qry, st, _NEG)
        m_old = m_scr[...]
        m_new = jnp.maximum(m_old, jnp.max(st, axis=0, keepdims=True))
        alpha = jnp.exp2(m_old - m_new)
        pt = jnp.exp2(st - m_new)
        l_scr[...] = alpha * l_scr[...] + jnp.sum(pt, axis=0, keepdims=True)
        acc_scr[...] = alpha * acc_scr[...] + _dot(vt_ref[:, pl.ds(off, tq)], pt.astype(_BF16))
        m_scr[...] = m_new

    def pair(i, c):
        tile(2 * i, False)
        tile(2 * i + 1, False)
        return c

    lax.fori_loop(0, qi // 2, pair, 0)

    @pl.when(qi % 2 == 1)
    def _():
        tile(qi - 1, False)

    tile(qi, True)
    o_ref[0] = (acc_scr[...] / l_scr[...]).T.astype(o_ref.dtype)


def _fox_prompt(q, aq, k, ak, vt):
    b, s, fw = q.shape
    hd = _HEAD_DIM
    tq = _tile(s, 512)
    qblk = lambda i, h, t: (i, t, h)
    kblk = lambda i, h, t: (i, 0, h)
    return pl.pallas_call(
        functools.partial(_fox_prompt_body, tq=tq),
        grid=(b, _HEADS, s // tq),
        in_specs=[
            pl.BlockSpec((1, tq, hd), qblk), pl.BlockSpec((1, tq, hd), qblk),
            pl.BlockSpec((1, s, hd), kblk), pl.BlockSpec((1, s, hd), kblk),
            pl.BlockSpec((hd, s), lambda i, h, t: (h, i)),
        ],
        out_specs=pl.BlockSpec((1, tq, hd), qblk),
        out_shape=jax.ShapeDtypeStruct((b, s, fw), _BF16),
        scratch_shapes=[
            pltpu.VMEM((s, 2 * hd), _BF16),
            pltpu.VMEM((1, tq), _F32), pltpu.VMEM((1, tq), _F32), pltpu.VMEM((hd, tq), _F32),
        ],
        compiler_params=_params("arbitrary", "arbitrary", "arbitrary"),
        name="fox_prompt",
    )(q, aq, k, ak, vt)


def _fox_decode_body(q_ref, kn_ref, vn_ref, lf_ref, lft_ref, ck_ref, cv_ref, clft_ref, o_ref,
                     carry_scr, m_scr, l_scr, acc_scr):
    kt = pl.program_id(1)
    nkt = pl.num_programs(1)
    hd = _HEAD_DIM
    t_new = q_ref.shape[1]
    tk = ck_ref.shape[1]

    lf = lf_ref[0]
    step = lax.broadcasted_iota(jnp.int32, lf.shape, 0)
    e_col = jnp.zeros_like(lf)
    for i in range(t_new):
        e_col = e_col + jnp.where(step >= i, lf[i:i + 1, :], 0.0)
    e_col = e_col * _LOG2E

    @pl.when(kt == 0)
    def _():
        carry_scr[...] = jnp.zeros_like(carry_scr)
        m_scr[...] = jnp.full_like(m_scr, _NEG)
        l_scr[...] = jnp.zeros_like(l_scr)
        acc_scr[...] = jnp.zeros_like(acc_scr)

    def update(h, s, vt):
        m_old = m_scr[h]
        m_new = jnp.maximum(m_old, jnp.max(s, axis=-1, keepdims=True))
        alpha = jnp.exp2(m_old - m_new)
        p = jnp.exp2(s - m_new)
        l_scr[h] = alpha * l_scr[h] + jnp.sum(p, axis=-1, keepdims=True)
        acc_scr[h] = alpha * acc_scr[h] + _dot(p.astype(_BF16), vt)
        m_scr[h] = m_new

    clft = clft_ref[0]
    after = (lax.broadcasted_iota(jnp.int32, (tk, tk), 0) > lax.broadcasted_iota(jnp.int32, (tk, tk), 1)).astype(_F32)
    d_row = jnp.dot(clft, after, precision=lax.Precision.HIGHEST, preferred_element_type=_F32) + carry_scr[:, 0:1]
    d_row = d_row * _LOG2E
    carry_scr[...] = carry_scr[...] + jnp.sum(clft, axis=-1, keepdims=True)
    for h in range(_HEADS):
        sl = slice(h * hd, (h + 1) * hd)
        s = _dot_nt(q_ref[0, :, sl], ck_ref[0, :, sl].astype(_BF16)) + e_col[:, h:h + 1] + d_row[h:h + 1, :]
        update(h, s, cv_ref[0, :, sl].astype(_BF16))

    @pl.when(kt == nkt - 1)
    def _():
        lft = lft_ref[0]
        pos = lax.broadcasted_iota(jnp.int32, lft.shape, 1)
        e_row = jnp.zeros_like(lft)
        for i in range(t_new):
            e_row = e_row + jnp.where(pos >= i, lft[:, i:i + 1], 0.0)
        e_row = e_row * _LOG2E
        row = lax.broadcasted_iota(jnp.int32, (t_new, t_new), 0)
        col = lax.broadcasted_iota(jnp.int32, (t_new, t_new), 1)
        for h in range(_HEADS):
            sl = slice(h * hd, (h + 1) * hd)
            s = _dot_nt(q_ref[0, :, sl], kn_ref[0, :, sl].astype(_BF16)) + e_col[:, h:h + 1] - e_row[h:h + 1, :]
            update(h, jnp.where(col <= row, s, _NEG), vn_ref[0, :, sl].astype(_BF16))
            o_ref[0, :, sl] = (acc_scr[h] / l_scr[h]).astype(o_ref.dtype)


def _fox_decode(q, kn, vn, lf, lft, ck, cv, clft):
    b, t, fw = q.shape
    p = ck.shape[1]
    tk = _tile(p, 512)
    nkt = p // tk
    new = lambda i, c: (i, 0, 0)
    return pl.pallas_call(
        _fox_decode_body,
        grid=(b, nkt),
        in_specs=[
            pl.BlockSpec((1, t, fw), new), pl.BlockSpec((1, t, fw), new), pl.BlockSpec((1, t, fw), new),
            pl.BlockSpec((1, t, lf.shape[2]), new), pl.BlockSpec((1, _HEADS, t), new),
            pl.BlockSpec((1, tk, fw), lambda i, c: (i, nkt - 1 - c, 0)),
            pl.BlockSpec((1, tk, fw), lambda i, c: (i, nkt - 1 - c, 0)),
            pl.BlockSpec((1, _HEADS, tk), lambda i, c: (i, 0, nkt - 1 - c)),
        ],
        out_specs=pl.BlockSpec((1, t, fw), new),
        out_shape=jax.ShapeDtypeStruct((b, t, fw), _BF16),
        scratch_shapes=[
            pltpu.VMEM((_HEADS, _V7X_LANES), _F32),
            pltpu.VMEM((_HEADS, t, 1), _F32), pltpu.VMEM((_HEADS, t, 1), _F32),
            pltpu.VMEM((_HEADS, t, _HEAD_DIM), _F32),
        ],
        compiler_params=_params("arbitrary", "arbitrary"),
        name="fox_decode",
    )(q, kn, vn, lf, lft, ck, cv, clft)


def _merge_body(x_ref, oa_ref, ob_ref, ga_ref, gb_ref, wa_ref, wb_ref, wo_ref, g2_ref, x1_ref, xn_ref, xnt_ref):
    y = ga_ref[...].astype(_F32) * _dot(oa_ref[...], wa_ref[...]) + gb_ref[...].astype(_F32) * _dot(ob_ref[...], wb_ref[...])
    x1 = x_ref[...] + _dot(y.astype(_BF16), wo_ref[...])
    x1_ref[...] = x1
    xn = _rms(x1) * g2_ref[...]
    xn_ref[...] = xn.astype(_BF16)
    xnt_ref[...] = xn.T.astype(_BF16)


def _merge(x, oa, ob, ga, gb, wa, wb, wo, g2):
    n, d = x.shape
    fw = oa.shape[1]
    tm = _tile(n, 256)
    row = lambda i: (i, 0)
    const = lambda i: (0, 0)
    once = pl.Buffered(1)
    return pl.pallas_call(
        _merge_body,
        grid=(n // tm,),
        in_specs=[
            pl.BlockSpec((tm, d), row), pl.BlockSpec((tm, fw), row), pl.BlockSpec((tm, fw), row),
            pl.BlockSpec((tm, d), row), pl.BlockSpec((tm, d), row),
            pl.BlockSpec((fw, d), const, pipeline_mode=once), pl.BlockSpec((fw, d), const, pipeline_mode=once),
            pl.BlockSpec((d, d), const, pipeline_mode=once), pl.BlockSpec((1, d), const),
        ],
        out_specs=[pl.BlockSpec((tm, d), row), pl.BlockSpec((tm, d), row), pl.BlockSpec((d, tm), lambda i: (0, i))],
        out_shape=[jax.ShapeDtypeStruct((n, d), _F32), jax.ShapeDtypeStruct((n, d), _BF16),
                   jax.ShapeDtypeStruct((d, n), _BF16)],
        compiler_params=_params("arbitrary"),
        name="merge",
    )(x, oa, ob, ga, gb, wa, wb, wo, g2)


def _peer_scores_body(xn_ref, wq_ref, pg_ref, keys_ref, s1_ref, s2_ref):
    pq = _dot(xn_ref[...], wq_ref[...])
    kd = pg_ref.shape[1]
    half = kd // 2
    for h in range(_HEADS):
        qn = (_rms(pq[:, h * kd:(h + 1) * kd]) * pg_ref[...]).astype(_BF16)
        s1_ref[h] = _dot_nt(keys_ref[h, 0], qn[:, :half]) * _LOG2E
        s2_ref[h] = _dot_nt(keys_ref[h, 1], qn[:, half:]) * _LOG2E


def _peer_scores(xn, wq, pg, keys):
    n, d = xn.shape
    nk = keys.shape[2]
    tm = _tile(n, 512)
    return pl.pallas_call(
        _peer_scores_body,
        grid=(n // tm,),
        in_specs=[
            pl.BlockSpec((tm, d), lambda i: (i, 0)),
            pl.BlockSpec(wq.shape, lambda i: (0, 0), pipeline_mode=pl.Buffered(1)),
            pl.BlockSpec(pg.shape, lambda i: (0, 0)),
            pl.BlockSpec(keys.shape, lambda i: (0, 0, 0, 0)),
        ],
        out_specs=[pl.BlockSpec((_HEADS, nk, tm), lambda i: (0, 0, i))] * 2,
        out_shape=[jax.ShapeDtypeStruct((_HEADS, nk, n), _F32)] * 2,
        compiler_params=_params("arbitrary"),
        name="peer_scores",
    )(xn, wq, pg, keys)


def _top_rows(a, k):
    out = jnp.full((k, a.shape[1]), _NEG, _F32)
    slot = lax.broadcasted_iota(jnp.int32, out.shape, 0)
    for r in range(k):
        m = jnp.max(a, axis=0, keepdims=True)
        out = jnp.where(slot == r, m, out)
        a = jnp.where(a == m, _NEG, a)
    return out


def _pair_sums(t1, t2):
    half = _PEER_TOPK // 2
    parts = [t2 + t1[0:1, :]]
    parts += [t2[:half, :] + t1[a:a + 1, :] for a in range(1, half)]
    parts.append(t2[0:1, :] + t1[half:, :])
    return jnp.concatenate(parts, axis=0)


def _peer_topk_body(s1_ref, s2_ref, s1c_ref, tau_ref):
    k = _PEER_TOPK
    for h in range(_HEADS):
        s1 = s1_ref[h]
        t1 = _top_rows(s1, k)
        t2 = _top_rows(s2_ref[h], k)
        top = _top_rows(_pair_sums(t1, t2), k)
        z = jnp.sum(jnp.exp2(top - top[0:1, :]), axis=0, keepdims=True)
        cz = top[0:1, :] + jnp.log2(z)
        s1c_ref[h] = s1 - cz
        tau_ref[h:h + 1, :] = _top_rows(_pair_sums(t1 - cz, t2), k)[k - 1:k, :]


def _peer_topk(s1, s2):
    heads, nk, n = s1.shape
    tl = _tile(n, 256)
    sblk = pl.BlockSpec((heads, nk, tl), lambda i: (0, 0, i))
    oblk = pl.BlockSpec((heads, tl), lambda i: (0, i))
    return pl.pallas_call(
        _peer_topk_body,
        grid=(n // tl,),
        in_specs=[sblk, sblk],
        out_specs=[sblk, oblk],
        out_shape=[jax.ShapeDtypeStruct((heads, nk, n), _F32), jax.ShapeDtypeStruct((heads, n), _F32)],
        compiler_params=_params("arbitrary"),
        name="peer_topk",
    )(s1, s2)


def _peer_dense_body(x1_ref, xnt_ref, s1a_ref, s1b_ref, s2_ref, tau_ref, u_ref, vt_ref, o_ref,
                     acc_scr, w_scr, wa_scr, s1x_scr, *, pieces, tl):
    j = pl.program_id(1)
    _, te, tm = w_scr.shape
    nk = s2_ref.shape[1]
    groups = te // nk
    d = acc_scr.shape[0]
    cur = j % 2
    nxt = 1 - cur

    def expand(s1_ref):
        for h in range(_HEADS):
            for r in range(groups):
                s1x_scr[h, r] = jnp.broadcast_to(s1_ref[h, r:r + 1, :], (8, tm))

    def gate_rows(r, slot):
        for lc in range(tm // tl):
            ls = slice(lc * tl, (lc + 1) * tl)
            w = jnp.zeros((nk, tl), _F32)
            for h in range(_HEADS):
                y = s2_ref[h, :, ls] + s1x_scr[h, r, 0:1, ls]
                w = w + jnp.where(y >= tau_ref[h:h + 1, ls], jnp.exp2(y), 0.0)
            w_scr[slot, pl.ds(pl.multiple_of(r * nk, nk), nk), ls] = w

    @pl.when(j == 0)
    def _():
        acc_scr[...] = jnp.zeros_like(acc_scr)
        wa_scr[1] = jnp.zeros((te, tm), _BF16)
        expand(s1a_ref)

        def first(r, c):
            gate_rows(r, 0)
            return c

        lax.fori_loop(0, groups, first, 0)

    expand(s1b_ref)
    ru, rd, gr = te // pieces, d // pieces, groups // pieces

    def piece(p, c):
        rows_d = pl.ds(pl.multiple_of(p * rd, rd), rd)
        acc_scr[rows_d, :] += _dot(vt_ref[rows_d, :], wa_scr[nxt])
        rows_u = pl.ds(pl.multiple_of(p * ru, ru), ru)
        a = jax.nn.gelu(_dot(u_ref[rows_u, :], xnt_ref[...]))
        wa_scr[cur, rows_u, :] = (w_scr[cur, rows_u, :] * a).astype(_BF16)
        for rr in range(gr):
            gate_rows(p * gr + rr, nxt)
        return c

    lax.fori_loop(0, pieces, piece, 0)

    @pl.when(j == pl.num_programs(1) - 1)
    def _():
        o_ref[...] = x1_ref[...] + acc_scr[...].T


def _peer_dense(x1, xnt, s1, s2, tau, tu, tvt):
    n, d = x1.shape
    heads, nk, _ = s1.shape
    ne = tu.shape[0]
    tm = _tile(n, 512)
    groups = 8
    te = groups * nk
    tok = lambda t, j: (t, 0)
    once = pl.Buffered(1)
    nj = ne // te
    s1_first = pl.BlockSpec((heads, groups, tm), lambda t, j: (0, 0, t))
    s1_next = pl.BlockSpec((heads, groups, tm), lambda t, j: (0, jnp.minimum(j + 1, nj - 1), t))
    return pl.pallas_call(
        functools.partial(_peer_dense_body, pieces=4, tl=_V7X_LANES),
        grid=(n // tm, nj + 1),
        in_specs=[
            pl.BlockSpec((tm, d), tok, pipeline_mode=once),
            pl.BlockSpec((d, tm), lambda t, j: (0, t), pipeline_mode=once),
            s1_first, s1_next,
            pl.BlockSpec((heads, nk, tm), lambda t, j: (0, 0, t), pipeline_mode=once),
            pl.BlockSpec((heads, tm), lambda t, j: (0, t)),
            pl.BlockSpec((te, d), lambda t, j: (jnp.minimum(j, nj - 1), 0)),
            pl.BlockSpec((d, te), lambda t, j: (0, jnp.maximum(j - 1, 0))),
        ],
        out_specs=pl.BlockSpec((tm, d), tok),
        out_shape=jax.ShapeDtypeStruct((n, d), _F32),
        scratch_shapes=[
            pltpu.VMEM((d, tm), _F32), pltpu.VMEM((2, te, tm), _F32), pltpu.VMEM((2, te, tm), _BF16),
            pltpu.VMEM((heads, groups, 8, tm), _F32),
        ],
        compiler_params=_params("arbitrary", "arbitrary"),
        name="peer_dense",
    )(x1, xnt, s1, s1, s2, tau, tu, tvt)


def _cast_tables_body(u_ref, v_ref, ub_ref, vtb_ref):
    ub_ref[...] = u_ref[0].astype(_BF16)
    vtb_ref[...] = v_ref[0].T.astype(_BF16)


def _cast_tables(l, peer_u, peer_v):
    _, ne, d = peer_u.shape
    tr = _tile(ne, 512)
    src = pl.BlockSpec((1, tr, d), lambda i: (l, i, 0))
    return pl.pallas_call(
        _cast_tables_body,
        grid=(ne // tr,),
        in_specs=[src, src],
        out_specs=[pl.BlockSpec((tr, d), lambda i: (i, 0)), pl.BlockSpec((d, tr), lambda i: (0, i))],
        out_shape=[jax.ShapeDtypeStruct((ne, d), _BF16), jax.ShapeDtypeStruct((d, ne), _BF16)],
        compiler_params=_params("arbitrary"),
        name="cast_tables",
    )(peer_u, peer_v)


def _prep_layer(l, norm_mix_g, w_in, b_forget, q_norm_g, k_norm_g, v_norm_g, w_spatial, b_spatial,
                w_out_a, w_out_b, w_out, norm_ffn_g, w_peer_q, peer_q_norm_g, peer_sub_keys, peer_u, peer_v):
    fw = _HEADS * _HEAD_DIM
    w = w_in[l]
    o3 = 3 * fw
    o4 = o3 + _HEADS
    w_main = jnp.concatenate([w[:, :o3], w[:, o4:]], axis=1).astype(_BF16)
    wf = jnp.pad(w[:, o3:o4], ((0, 0), (0, _V7X_LANES - _HEADS))).astype(_BF16)
    bfp = jnp.pad(b_forget[l], (0, _V7X_LANES - _HEADS)).reshape(1, _V7X_LANES)
    tu, tvt = _cast_tables(l, peer_u, peer_v)
    return dict(
        g1=norm_mix_g[l].reshape(1, -1), w_main=w_main, wf=wf, bfp=bfp,
        qg=q_norm_g[l].reshape(1, -1), kg=k_norm_g[l].reshape(1, -1), vg=v_norm_g[l].reshape(1, -1),
        w_spatial=w_spatial[l], b_spatial=b_spatial[l],
        wa=w_out_a[l].astype(_BF16), wb=w_out_b[l].astype(_BF16), wo=w_out[l].astype(_BF16),
        g2=norm_ffn_g[l].reshape(1, -1), wq=w_peer_q[l].astype(_BF16), pg=peer_q_norm_g[l].reshape(1, -1),
        keys=peer_sub_keys[l].astype(_BF16), tu=tu, tvt=tvt,
    )


def _sgu_weights(p, frames):
    reps = _SGU_CHUNK // frames
    w = p["w_spatial"][:, :frames, :frames]
    eye = jnp.eye(reps, dtype=w.dtype)
    ws = jnp.einsum("ab,gij->gaibj", eye, w).reshape(_HEADS, _SGU_CHUNK, _SGU_CHUNK).astype(_BF16)
    bs = jnp.tile(p["b_spatial"][:, :frames], (1, reps))
    bs = jnp.broadcast_to(bs[:, :, None], (_HEADS, _SGU_CHUNK, _HEAD_DIM))
    return ws, bs


def _peer(p, x1, xn, xnt):
    s1, s2 = _peer_scores(xn, p["wq"], p["pg"], p["keys"])
    s1c, tau = _peer_topk(s1, s2)
    return _peer_dense(x1, xnt, s1c, s2, tau, p["tu"], p["tvt"])


def _layer_prompt(p, xp):
    b, s, d = xp.shape
    fw = _HEADS * _HEAD_DIM
    x = xp.reshape(b * s, d)
    ws, bs = _sgu_weights(p, _SGU_CHUNK)
    q, k, v, lf, ob, ga, gb, vt = _in_proj(x, p["g1"], p["w_main"], p["wf"], p["bfp"], p["qg"], p["kg"], p["vg"],
                                           ws, bs, prompt=True)
    aq, ak = _fcum_aug(lf.reshape(b, s, -1))
    oa = _fox_prompt(q.reshape(b, s, fw), aq, k.reshape(b, s, fw), ak, vt)
    x1, xn, xnt = _merge(x, oa.reshape(b * s, fw), ob, ga, gb, p["wa"], p["wb"], p["wo"], p["g2"])
    y = _peer(p, x1, xn, xnt)
    return (y.reshape(b, s, d), k.reshape(b, s, _HEADS, _HEAD_DIM), v.reshape(b, s, _HEADS, _HEAD_DIM),
            lf[:, :_HEADS].reshape(b, s, _HEADS))


def _layer_sample(p, xs, ck, cv, clf):
    b, t, d = xs.shape
    fw = _HEADS * _HEAD_DIM
    past = ck.shape[1]
    x = xs.reshape(b * t, d)
    ws, bs = _sgu_weights(p, t)
    q, k, v, lf, ob, ga, gb, sv = _in_proj(x, p["g1"], p["w_main"], p["wf"], p["bfp"], p["qg"], p["kg"], p["vg"],
                                           ws, bs, prompt=False)
    lf3 = lf.reshape(b, t, -1)
    lft = jnp.swapaxes(lf3[:, :, :_HEADS], 1, 2)
    clft = jnp.swapaxes(clf.astype(_F32), 1, 2)
    oa = _fox_decode(q.reshape(b, t, fw), k.reshape(b, t, fw), v.reshape(b, t, fw), lf3, lft,
                     ck.reshape(b, past, fw), cv.reshape(b, past, fw), clft)
    x1, xn, xnt = _merge(x, oa.reshape(b * t, fw), ob, ga, gb, p["wa"], p["wb"], p["wo"], p["g2"])
    y = _peer(p, x1, xn, xnt)
    return (y.reshape(b, t, d), k.reshape(b, t, _HEADS, _HEAD_DIM), v.reshape(b, t, _HEADS, _HEAD_DIM),
            lf[:, :_HEADS].reshape(b, t, _HEADS), sv.reshape(b, t, fw))


def kernel(x_prompt, x_sample, cache_k, cache_v, cache_logf, norm_mix_g, w_in, b_forget, q_norm_g, k_norm_g, v_norm_g, w_spatial, b_spatial, w_out_a, w_out_b, w_out, norm_ffn_g, w_peer_q, peer_q_norm_g, peer_sub_keys, peer_u, peer_v):
    xp, xs = x_prompt, x_sample
    kp_l, vp_l, fp_l, ks_l, vs_l, fs_l, us_l = [], [], [], [], [], [], []
    for l in range(w_in.shape[0]):
        p = _prep_layer(l, norm_mix_g, w_in, b_forget, q_norm_g, k_norm_g, v_norm_g, w_spatial, b_spatial,
                        w_out_a, w_out_b, w_out, norm_ffn_g, w_peer_q, peer_q_norm_g, peer_sub_keys, peer_u, peer_v)
        xp, k, v, f = _layer_prompt(p, xp)
        kp_l.append(k)
        vp_l.append(v)
        fp_l.append(f)
        xs, k, v, f, u = _layer_sample(p, xs, cache_k[l], cache_v[l], cache_logf[l])
        ks_l.append(k)
        vs_l.append(v)
        fs_l.append(f)
        us_l.append(u)
    return (xp, xs, jnp.stack(kp_l), jnp.stack(vp_l), jnp.stack(fp_l),
            jnp.stack(ks_l), jnp.stack(vs_l), jnp.stack(fs_l), jnp.stack(us_l))
```

```python
import functools

import jax
import jax.numpy as jnp
from jax import lax
from jax.experimental import pallas as pl
from jax.experimental.pallas import tpu as pltpu

_F32 = jnp.float32
_BF16 = jnp.bfloat16

_V7X_LANES = 128
_V7X_VMEM_BYTES = 64 * 1024 * 1024
_VMEM_LIMIT = _V7X_VMEM_BYTES - 8 * 1024 * 1024

_RMS_EPS = 1e-6
_NEG = -1e30
_HEADS = 8
_HEAD_DIM = 128
_FOX_SCALE = _HEAD_DIM ** -0.5
_LOG2E = 1.4426950408889634
_PEER_TOPK = 16
_SGU_CHUNK = 128


def _params(*sem):
    return pltpu.CompilerParams(dimension_semantics=sem, vmem_limit_bytes=_VMEM_LIMIT)


def _rms(x):
    return x * lax.rsqrt(jnp.mean(x * x, axis=-1, keepdims=True) + _RMS_EPS)


def _log_sigmoid(x):
    return jnp.minimum(x, 0.0) - jnp.log1p(jnp.exp(-jnp.abs(x)))


def _dot(a, b):
    return jnp.dot(a, b, preferred_element_type=_F32)


def _dot_nt(a, b):
    return lax.dot_general(a, b, (((1,), (1,)), ((), ())), preferred_element_type=_F32)


def _tile(n, pref):
    t = min(n, pref)
    while n % t:
        t //= 2
    return t


def _in_proj_body(x_ref, g_ref, w_ref, wf_ref, bf_ref, qg_ref, kg_ref, vg_ref, ws_ref, bs_ref,
                  q_ref, k_ref, v_ref, lf_ref, ob_ref, ga_ref, gb_ref, extra_ref, hn_scr, u_scr, *, prompt):
    j = pl.program_id(1)
    hd = _HEAD_DIM

    @pl.when(j == 0)
    def _():
        hn = (_rms(x_ref[...]) * g_ref[...]).astype(_BF16)
        hn_scr[...] = hn
        f = _dot(hn, wf_ref[...]) + bf_ref[...]
        lane = lax.broadcasted_iota(jnp.int32, f.shape, 1)
        lf_ref[...] = jnp.where(lane < _HEADS, _log_sigmoid(f), 0.0)

    def proj():
        return _dot(hn_scr[...], w_ref[...])

    @pl.when(j == 0)
    def _():
        z = proj()
        for h in range(_HEADS):
            sl = slice(h * hd, (h + 1) * hd)
            q_ref[:, sl] = (_rms(z[:, sl]) * (qg_ref[...] * (_FOX_SCALE * _LOG2E))).astype(_BF16)

    @pl.when(j == 1)
    def _():
        z = proj()
        for h in range(_HEADS):
            sl = slice(h * hd, (h + 1) * hd)
            k_ref[:, sl] = _rms(z[:, sl]) * kg_ref[...]

    @pl.when(j == 2)
    def _():
        z = proj()
        v_ref[...] = z
        if prompt:
            extra_ref[...] = z.T.astype(_BF16)

    @pl.when(j == 3)
    def _():
        u_scr[...] = jax.nn.gelu(proj())

    @pl.when(j == 4)
    def _():
        z = proj()
        vs = _rms(jax.nn.gelu(z)) * vg_ref[...]
        if not prompt:
            extra_ref[...] = vs
        c = _SGU_CHUNK
        row = lax.broadcasted_iota(jnp.int32, (c, c), 0)
        col = lax.broadcasted_iota(jnp.int32, (c, c), 1)
        for g in range(_HEADS):
            w = jnp.where(row >= col, ws_ref[g], jnp.zeros((), _BF16))
            for t in range(z.shape[0] // c):
                rs, cs = slice(t * c, (t + 1) * c), slice(g * hd, (g + 1) * hd)
                mixed = _dot(w, vs[rs, cs].astype(_BF16)) + bs_ref[g]
                ob_ref[rs, cs] = (u_scr[rs, cs] * mixed).astype(_BF16)

    @pl.when((j == 5) | (j == 6))
    def _():
        ga_ref[...] = jax.nn.sigmoid(proj()).astype(_BF16)

    @pl.when(j >= 7)
    def _():
        gb_ref[...] = jax.nn.sigmoid(proj()).astype(_BF16)


def _in_proj(x, g, w_main, wf, bfp, qg, kg, vg, ws, bs, *, prompt):
    n, d = x.shape
    fw = _HEADS * _HEAD_DIM
    tm = _tile(n, 512)
    nblk = w_main.shape[1] // fw
    const = lambda i, j: (0, 0)
    row = lambda i, j: (i, 0)
    out_shape = [
        jax.ShapeDtypeStruct((n, fw), _BF16),
        jax.ShapeDtypeStruct((n, fw), _F32),
        jax.ShapeDtypeStruct((n, fw), _F32),
        jax.ShapeDtypeStruct((n, _V7X_LANES), _F32),
        jax.ShapeDtypeStruct((n, fw), _BF16),
        jax.ShapeDtypeStruct((n, 2 * fw), _BF16),
        jax.ShapeDtypeStruct((n, 2 * fw), _BF16),
    ]
    out_specs = [
        pl.BlockSpec((tm, fw), row), pl.BlockSpec((tm, fw), row), pl.BlockSpec((tm, fw), row),
        pl.BlockSpec((tm, _V7X_LANES), row), pl.BlockSpec((tm, fw), row),
        pl.BlockSpec((tm, fw), lambda i, j: (i, jnp.clip(j - 5, 0, 1))),
        pl.BlockSpec((tm, fw), lambda i, j: (i, jnp.clip(j - 7, 0, 1))),
    ]
    if prompt:
        out_shape.append(jax.ShapeDtypeStruct((fw, n), _BF16))
        out_specs.append(pl.BlockSpec((fw, tm), lambda i, j: (0, i)))
    else:
        out_shape.append(jax.ShapeDtypeStruct((n, fw), _F32))
        out_specs.append(pl.BlockSpec((tm, fw), row))
    return pl.pallas_call(
        functools.partial(_in_proj_body, prompt=prompt),
        grid=(n // tm, nblk),
        in_specs=[
            pl.BlockSpec((tm, d), row),
            pl.BlockSpec((1, d), const),
            pl.BlockSpec((d, fw), lambda i, j: (0, j)),
            pl.BlockSpec((d, _V7X_LANES), const),
            pl.BlockSpec((1, _V7X_LANES), const),
            pl.BlockSpec((1, _HEAD_DIM), const),
            pl.BlockSpec((1, _HEAD_DIM), const),
            pl.BlockSpec((1, fw), const),
            pl.BlockSpec(ws.shape, lambda i, j: (0, 0, 0)),
            pl.BlockSpec(bs.shape, lambda i, j: (0, 0, 0)),
        ],
        out_specs=out_specs,
        out_shape=out_shape,
        scratch_shapes=[pltpu.VMEM((tm, d), _BF16), pltpu.VMEM((tm, fw), _F32)],
        compiler_params=_params("arbitrary", "arbitrary"),
        name="in_proj",
    )(x, g, w_main, wf, bfp, qg, kg, vg, ws, bs)


def _split3(x):
    hi = x.astype(_BF16)
    r1 = x - hi.astype(_F32)
    mid = r1.astype(_BF16)
    lo = (r1 - mid.astype(_F32)).astype(_BF16)
    return hi, mid, lo


def _fcum_aug_body(lf_ref, aq_ref, ak_ref, carry_scr):
    @pl.when(pl.program_id(1) == 0)
    def _():
        carry_scr[...] = jnp.zeros_like(carry_scr)

    lf = lf_ref[0]
    n = lf.shape[0]
    tri = (lax.broadcasted_iota(jnp.int32, (n, n), 0) >= lax.broadcasted_iota(jnp.int32, (n, n), 1)).astype(_F32)
    cs = jnp.dot(tri, lf, precision=lax.Precision.HIGHEST, preferred_element_type=_F32) + carry_scr[0:1, :]
    carry_scr[...] = jnp.broadcast_to(cs[n - 1:n, :], carry_scr.shape)
    pieces = _split3(cs * _LOG2E)

    fw = _HEADS * _HEAD_DIM
    src = lax.broadcasted_iota(jnp.int32, (_V7X_LANES, fw), 0)
    dst = lax.broadcasted_iota(jnp.int32, (_V7X_LANES, fw), 1)

    def place(p):
        return jnp.where(dst == src * _HEAD_DIM + p, 1.0, 0.0).astype(_BF16)

    fq = _dot(pieces[0], place(0)) + _dot(pieces[1], place(1)) + _dot(pieces[2], place(2))
    fk = _dot(pieces[0], place(3)) + _dot(pieces[1], place(4)) + _dot(pieces[2], place(5))
    lane = lax.broadcasted_iota(jnp.int32, (1, fw), 1) % _HEAD_DIM
    aq_ref[0] = (fq + jnp.where((lane >= 3) & (lane < 6), 1.0, 0.0)).astype(_BF16)
    ak_ref[0] = (jnp.where(lane < 3, 1.0, 0.0) - fk).astype(_BF16)


def _fcum_aug(lf):
    b, s, lanes = lf.shape
    fw = _HEADS * _HEAD_DIM
    tl = _tile(s, 512)
    blk = lambda i, c: (i, c, 0)
    return pl.pallas_call(
        _fcum_aug_body,
        grid=(b, s // tl),
        in_specs=[pl.BlockSpec((1, tl, lanes), blk)],
        out_specs=[pl.BlockSpec((1, tl, fw), blk), pl.BlockSpec((1, tl, fw), blk)],
        out_shape=[jax.ShapeDtypeStruct((b, s, fw), _BF16)] * 2,
        scratch_shapes=[pltpu.VMEM((8, lanes), _F32)],
        compiler_params=_params("arbitrary", "arbitrary"),
        name="fcum_aug",
    )(lf)


def _fox_prompt_body(q_ref, aq_ref, k_ref, ak_ref, vt_ref, o_ref, kc_scr, m_scr, l_scr, acc_scr, s_scr, *, tq):
    qi = pl.program_id(2)
    hd = _HEAD_DIM

    @pl.when(qi == 0)
    def _():
        kc_scr[:, :hd] = k_ref[0].astype(_BF16)
        kc_scr[:, hd:] = ak_ref[0]

    qt = jnp.concatenate([q_ref[0].astype(_F32).T, aq_ref[0].astype(_F32).T], axis=0).astype(_BF16)
    m_scr[...] = jnp.full_like(m_scr, _NEG)
    l_scr[...] = jnp.zeros_like(l_scr)
    acc_scr[...] = jnp.zeros_like(acc_scr)

    def scores(kj):
        return _dot(kc_scr[pl.ds(pl.multiple_of(kj * tq, tq), tq), :], qt)

    def absorb(kj, st, diag):
        if diag:
            key = lax.broadcasted_iota(jnp.int32, st.shape, 0)
            qry = lax.broadcasted_iota(jnp.int32, st.shape, 1)
            st = jnp.where(key <= qry, st, _NEG)
        m_old = m_scr[...]
        m_new = jnp.maximum(m_old, jnp.max(st, axis=0, keepdims=True))
        alpha = jnp.exp2(m_old - m_new)
        pt = jnp.exp2(st - m_new)
        l_scr[...] = alpha * l_scr[...] + jnp.sum(pt, axis=0, keepdims=True)
        acc_scr[...] = alpha * acc_scr[...] + _dot(vt_ref[:, pl.ds(pl.multiple_of(kj * tq, tq), tq)], pt.astype(_BF16))
        m_scr[...] = m_new

    def tile(kj, diag):
        absorb(kj, scores(kj), diag)

    group = s_scr.shape[0]

    def parked(first, n):
        for g in range(n):
            s_scr[g] = scores(first + g)
        for g in range(n):
            absorb(first + g, s_scr[g], False)

    def many(i, c):
        parked(group * i, group)
        return c

    full = qi // group
    lax.fori_loop(0, full, many, 0)
    rest = qi - full * group

    @pl.when(rest >= 2)
    def _():
        parked(full * group, 2)

    @pl.when(rest % 2 == 1)
    def _():
        tile(qi - 1, False)

    tile(qi, True)
    o_ref[0] = (acc_scr[...] / l_scr[...]).T.astype(o_ref.dtype)


def _fox_prompt(q, aq, k, ak, vt):
    b, s, fw = q.shape
    hd = _HEAD_DIM
    tq = _tile(s, 512)
    qblk = lambda i, h, t: (i, t, h)
    kblk = lambda i, h, t: (i, 0, h)
    return pl.pallas_call(
        functools.partial(_fox_prompt_body, tq=tq),
        grid=(b, _HEADS, s // tq),
        in_specs=[
            pl.BlockSpec((1, tq, hd), qblk), pl.BlockSpec((1, tq, hd), qblk),
            pl.BlockSpec((1, s, hd), kblk), pl.BlockSpec((1, s, hd), kblk),
            pl.BlockSpec((hd, s), lambda i, h, t: (h, i)),
        ],
        out_specs=pl.BlockSpec((1, tq, hd), qblk),
        out_shape=jax.ShapeDtypeStruct((b, s, fw), _BF16),
        scratch_shapes=[
            pltpu.VMEM((s, 2 * hd), _BF16),
            pltpu.VMEM((1, tq), _F32), pltpu.VMEM((1, tq), _F32), pltpu.VMEM((hd, tq), _F32),
            pltpu.VMEM((4, tq, tq), _F32),
        ],
        compiler_params=_params("arbitrary", "arbitrary", "arbitrary"),
        name="fox_prompt",
    )(q, aq, k, ak, vt)


def _fox_decode_body(q_ref, kn_ref, vn_ref, lf_ref, lft_ref, ck_ref, cv_ref, clft_ref, o_ref,
                     carry_scr, m_scr, l_scr, acc_scr):
    kt = pl.program_id(1)
    nkt = pl.num_programs(1)
    hd = _HEAD_DIM
    t_new = q_ref.shape[1]
    tk = clft_ref.shape[2]

    lf = lf_ref[0]
    step = lax.broadcasted_iota(jnp.int32, lf.shape, 0)
    e_col = jnp.zeros_like(lf)
    for i in range(t_new):
        e_col = e_col + jnp.where(step >= i, lf[i:i + 1, :], 0.0)
    e_col = e_col * _LOG2E

    @pl.when(kt == 0)
    def _():
        carry_scr[...] = jnp.zeros_like(carry_scr)
        m_scr[...] = jnp.full_like(m_scr, _NEG)
        l_scr[...] = jnp.zeros_like(l_scr)
        acc_scr[...] = jnp.zeros_like(acc_scr)

    def update(h, s, vt):
        m_old = m_scr[h]
        m_new = jnp.maximum(m_old, jnp.max(s, axis=-1, keepdims=True))
        alpha = jnp.exp2(m_old - m_new)
        p = jnp.exp2(s - m_new)
        l_scr[h] = alpha * l_scr[h] + jnp.sum(p, axis=-1, keepdims=True)
        acc_scr[h] = alpha * acc_scr[h] + _dot(p.astype(_BF16), vt)
        m_scr[h] = m_new

    clft = clft_ref[0]
    after = (lax.broadcasted_iota(jnp.int32, (tk, tk), 0) > lax.broadcasted_iota(jnp.int32, (tk, tk), 1)).astype(_F32)
    d_row = jnp.dot(clft, after, precision=lax.Precision.HIGHEST, preferred_element_type=_F32) + carry_scr[:, 0:1]
    d_row = d_row * _LOG2E
    carry_scr[...] = carry_scr[...] + jnp.sum(clft, axis=-1, keepdims=True)
    for h in range(_HEADS):
        sl = slice(h * hd, (h + 1) * hd)
        rows = pl.ds(h, tk, stride=_HEADS)
        s = _dot_nt(q_ref[0, :, sl], ck_ref[0, rows, :].astype(_BF16)) + e_col[:, h:h + 1] + d_row[h:h + 1, :]
        update(h, s, cv_ref[0, rows, :].astype(_BF16))

    @pl.when(kt == nkt - 1)
    def _():
        lft = lft_ref[0]
        pos = lax.broadcasted_iota(jnp.int32, lft.shape, 1)
        e_row = jnp.zeros_like(lft)
        for i in range(t_new):
            e_row = e_row + jnp.where(pos >= i, lft[:, i:i + 1], 0.0)
        e_row = e_row * _LOG2E
        row = lax.broadcasted_iota(jnp.int32, (t_new, t_new), 0)
        col = lax.broadcasted_iota(jnp.int32, (t_new, t_new), 1)
        for h in range(_HEADS):
            sl = slice(h * hd, (h + 1) * hd)
            s = _dot_nt(q_ref[0, :, sl], kn_ref[0, :, sl].astype(_BF16)) + e_col[:, h:h + 1] - e_row[h:h + 1, :]
            update(h, jnp.where(col <= row, s, _NEG), vn_ref[0, :, sl].astype(_BF16))
            o_ref[0, :, sl] = (acc_scr[h] / l_scr[h]).astype(o_ref.dtype)


def _fox_decode(l, q, kn, vn, lf, lft, cache_k, cache_v, clft):
    b, t, fw = q.shape
    p = cache_k.shape[2]
    ck = cache_k.reshape(-1, p * _HEADS, _HEAD_DIM)
    cv = cache_v.reshape(-1, p * _HEADS, _HEAD_DIM)
    tk = _tile(p, 512)
    nkt = p // tk
    new = lambda i, c: (i, 0, 0)
    cache = pl.BlockSpec((1, tk * _HEADS, _HEAD_DIM), lambda i, c: (l * b + i, nkt - 1 - c, 0))
    return pl.pallas_call(
        _fox_decode_body,
        grid=(b, nkt),
        in_specs=[
            pl.BlockSpec((1, t, fw), new), pl.BlockSpec((1, t, fw), new), pl.BlockSpec((1, t, fw), new),
            pl.BlockSpec((1, t, lf.shape[2]), new), pl.BlockSpec((1, _HEADS, t), new),
            cache, cache,
            pl.BlockSpec((1, _HEADS, tk), lambda i, c: (i, 0, nkt - 1 - c)),
        ],
        out_specs=pl.BlockSpec((1, t, fw), new),
        out_shape=jax.ShapeDtypeStruct((b, t, fw), _BF16),
        scratch_shapes=[
            pltpu.VMEM((_HEADS, _V7X_LANES), _F32),
            pltpu.VMEM((_HEADS, t, 1), _F32), pltpu.VMEM((_HEADS, t, 1), _F32),
            pltpu.VMEM((_HEADS, t, _HEAD_DIM), _F32),
        ],
        compiler_params=_params("arbitrary", "arbitrary"),
        name="fox_decode",
    )(q, kn, vn, lf, lft, ck, cv, clft)


def _merge_body(x_ref, oa_ref, ob_ref, ga_ref, gb_ref, wa_ref, wb_ref, wo_ref, g2_ref, x1_ref, xn_ref, xnt_ref):
    y = ga_ref[...].astype(_F32) * _dot(oa_ref[...], wa_ref[...]) + gb_ref[...].astype(_F32) * _dot(ob_ref[...], wb_ref[...])
    x1 = x_ref[...] + _dot(y.astype(_BF16), wo_ref[...])
    x1_ref[...] = x1
    xn = _rms(x1) * g2_ref[...]
    xn_ref[...] = xn.astype(_BF16)
    xnt_ref[...] = xn.T.astype(_BF16)


def _merge(x, oa, ob, ga, gb, wa, wb, wo, g2):
    n, d = x.shape
    fw = oa.shape[1]
    tm = _tile(n, 256)
    row = lambda i: (i, 0)
    const = lambda i: (0, 0)
    once = pl.Buffered(1)
    return pl.pallas_call(
        _merge_body,
        grid=(n // tm,),
        in_specs=[
            pl.BlockSpec((tm, d), row), pl.BlockSpec((tm, fw), row), pl.BlockSpec((tm, fw), row),
            pl.BlockSpec((tm, d), row), pl.BlockSpec((tm, d), row),
            pl.BlockSpec((fw, d), const, pipeline_mode=once), pl.BlockSpec((fw, d), const, pipeline_mode=once),
            pl.BlockSpec((d, d), const, pipeline_mode=once), pl.BlockSpec((1, d), const),
        ],
        out_specs=[pl.BlockSpec((tm, d), row), pl.BlockSpec((tm, d), row), pl.BlockSpec((d, tm), lambda i: (0, i))],
        out_shape=[jax.ShapeDtypeStruct((n, d), _F32), jax.ShapeDtypeStruct((n, d), _BF16),
                   jax.ShapeDtypeStruct((d, n), _BF16)],
        compiler_params=_params("arbitrary"),
        name="merge",
    )(x, oa, ob, ga, gb, wa, wb, wo, g2)


def _peer_scores_body(xn_ref, wq_ref, pg_ref, keys_ref, s1_ref, s2_ref):
    pq = _dot(xn_ref[...], wq_ref[...])
    kd = pg_ref.shape[1]
    half = kd // 2
    for h in range(_HEADS):
        qn = (_rms(pq[:, h * kd:(h + 1) * kd]) * pg_ref[...]).astype(_BF16)
        s1_ref[h] = _dot_nt(keys_ref[h, 0], qn[:, :half]) * _LOG2E
        s2_ref[h] = _dot_nt(keys_ref[h, 1], qn[:, half:]) * _LOG2E


def _peer_scores(xn, wq, pg, keys):
    n, d = xn.shape
    nk = keys.shape[2]
    tm = _tile(n, 512)
    return pl.pallas_call(
        _peer_scores_body,
        grid=(n // tm,),
        in_specs=[
            pl.BlockSpec((tm, d), lambda i: (i, 0)),
            pl.BlockSpec(wq.shape, lambda i: (0, 0), pipeline_mode=pl.Buffered(1)),
            pl.BlockSpec(pg.shape, lambda i: (0, 0)),
            pl.BlockSpec(keys.shape, lambda i: (0, 0, 0, 0)),
        ],
        out_specs=[pl.BlockSpec((_HEADS, nk, tm), lambda i: (0, 0, i))] * 2,
        out_shape=[jax.ShapeDtypeStruct((_HEADS, nk, n), _F32)] * 2,
        compiler_params=_params("arbitrary"),
        name="peer_scores",
    )(xn, wq, pg, keys)


def _top_rows(a, k):
    out = jnp.full((k, a.shape[1]), _NEG, _F32)
    slot = lax.broadcasted_iota(jnp.int32, out.shape, 0)
    for r in range(k):
        m = jnp.max(a, axis=0, keepdims=True)
        out = jnp.where(slot == r, m, out)
        a = jnp.where(a == m, _NEG, a)
    return out


def _pair_sums(t1, t2):
    half = _PEER_TOPK // 2
    parts = [t2 + t1[0:1, :]]
    parts += [t2[:half, :] + t1[a:a + 1, :] for a in range(1, half)]
    parts.append(t2[0:1, :] + t1[half:, :])
    return jnp.concatenate(parts, axis=0)


def _peer_topk_body(s1_ref, s2_ref, s1c_ref, tau_ref):
    k = _PEER_TOPK
    for h in range(_HEADS):
        s1 = s1_ref[h]
        t1 = _top_rows(s1, k)
        t2 = _top_rows(s2_ref[h], k)
        top = _top_rows(_pair_sums(t1, t2), k)
        z = jnp.sum(jnp.exp2(top - top[0:1, :]), axis=0, keepdims=True)
        cz = top[0:1, :] + jnp.log2(z)
        s1c_ref[h] = s1 - cz
        tau_ref[h:h + 1, :] = _top_rows(_pair_sums(t1 - cz, t2), k)[k - 1:k, :]


def _peer_topk(s1, s2):
    heads, nk, n = s1.shape
    tl = _tile(n, 256)
    sblk = pl.BlockSpec((heads, nk, tl), lambda i: (0, 0, i))
    oblk = pl.BlockSpec((heads, tl), lambda i: (0, i))
    return pl.pallas_call(
        _peer_topk_body,
        grid=(n // tl,),
        in_specs=[sblk, sblk],
        out_specs=[sblk, oblk],
        out_shape=[jax.ShapeDtypeStruct((heads, nk, n), _F32), jax.ShapeDtypeStruct((heads, n), _F32)],
        compiler_params=_params("arbitrary"),
        name="peer_topk",
    )(s1, s2)


def _peer_dense_body(x1_ref, xnt_ref, s1a_ref, s1b_ref, s2_ref, tau_ref, u_ref, vt_ref, o_ref,
                     acc_scr, w_scr, wa_scr, s1x_scr, *, pieces, tl):
    j = pl.program_id(1)
    _, te, tm = w_scr.shape
    nk = s2_ref.shape[1]
    groups = te // nk
    d = acc_scr.shape[0]
    cur = j % 2
    nxt = 1 - cur

    def expand(s1_ref):
        for h in range(_HEADS):
            for r in range(groups):
                s1x_scr[h, r] = jnp.broadcast_to(s1_ref[h, r:r + 1, :], (8, tm))

    def gate_rows(r, slot):
        for lc in range(tm // tl):
            ls = slice(lc * tl, (lc + 1) * tl)
            w = jnp.zeros((nk, tl), _F32)
            for h in range(_HEADS):
                y = s2_ref[h, :, ls] + s1x_scr[h, r, 0:1, ls]
                w = w + jnp.where(y >= tau_ref[h:h + 1, ls], jnp.exp2(y), 0.0)
            w_scr[slot, pl.ds(pl.multiple_of(r * nk, nk), nk), ls] = w

    @pl.when(j == 0)
    def _():
        acc_scr[...] = jnp.zeros_like(acc_scr)
        wa_scr[1] = jnp.zeros((te, tm), _BF16)
        expand(s1a_ref)

        def first(r, c):
            gate_rows(r, 0)
            return c

        lax.fori_loop(0, groups, first, 0)

    expand(s1b_ref)
    ru, rd, gr = te // pieces, d // pieces, groups // pieces

    def piece(p, c):
        rows_d = pl.ds(pl.multiple_of(p * rd, rd), rd)
        acc_scr[rows_d, :] += _dot(vt_ref[rows_d, :], wa_scr[nxt])
        rows_u = pl.ds(pl.multiple_of(p * ru, ru), ru)
        a = jax.nn.gelu(_dot(u_ref[rows_u, :], xnt_ref[...]))
        wa_scr[cur, rows_u, :] = (w_scr[cur, rows_u, :] * a).astype(_BF16)
        for rr in range(gr):
            gate_rows(p * gr + rr, nxt)
        return c

    lax.fori_loop(0, pieces, piece, 0)

    @pl.when(j == pl.num_programs(1) - 1)
    def _():
        o_ref[...] = x1_ref[...] + acc_scr[...].T


def _peer_dense(x1, xnt, s1, s2, tau, tu, tvt):
    n, d = x1.shape
    heads, nk, _ = s1.shape
    ne = tu.shape[0]
    tm = _tile(n, 512)
    groups = 8
    te = groups * nk
    tok = lambda t, j: (t, 0)
    once = pl.Buffered(1)
    nj = ne // te
    s1_first = pl.BlockSpec((heads, groups, tm), lambda t, j: (0, 0, t))
    s1_next = pl.BlockSpec((heads, groups, tm), lambda t, j: (0, jnp.minimum(j + 1, nj - 1), t))
    return pl.pallas_call(
        functools.partial(_peer_dense_body, pieces=4, tl=_V7X_LANES),
        grid=(n // tm, nj + 1),
        in_specs=[
            pl.BlockSpec((tm, d), tok, pipeline_mode=once),
            pl.BlockSpec((d, tm), lambda t, j: (0, t), pipeline_mode=once),
            s1_first, s1_next,
            pl.BlockSpec((heads, nk, tm), lambda t, j: (0, 0, t), pipeline_mode=once),
            pl.BlockSpec((heads, tm), lambda t, j: (0, t)),
            pl.BlockSpec((te, d), lambda t, j: (jnp.minimum(j, nj - 1), 0)),
            pl.BlockSpec((d, te), lambda t, j: (0, jnp.maximum(j - 1, 0))),
        ],
        out_specs=pl.BlockSpec((tm, d), tok),
        out_shape=jax.ShapeDtypeStruct((n, d), _F32),
        scratch_shapes=[
            pltpu.VMEM((d, tm), _F32), pltpu.VMEM((2, te, tm), _F32), pltpu.VMEM((2, te, tm), _BF16),
            pltpu.VMEM((heads, groups, 8, tm), _F32),
        ],
        compiler_params=_params("arbitrary", "arbitrary"),
        name="peer_dense",
    )(x1, xnt, s1, s1, s2, tau, tu, tvt)


def _cast_tables_body(u_ref, v_ref, ub_ref, vtb_ref):
    ub_ref[...] = u_ref[0].astype(_BF16)
    vtb_ref[...] = v_ref[0].T.astype(_BF16)


def _cast_tables(l, peer_u, peer_v):
    _, ne, d = peer_u.shape
    tr = _tile(ne, 512)
    src = pl.BlockSpec((1, tr, d), lambda i: (l, i, 0))
    return pl.pallas_call(
        _cast_tables_body,
        grid=(ne // tr,),
        in_specs=[src, src],
        out_specs=[pl.BlockSpec((tr, d), lambda i: (i, 0)), pl.BlockSpec((d, tr), lambda i: (0, i))],
        out_shape=[jax.ShapeDtypeStruct((ne, d), _BF16), jax.ShapeDtypeStruct((d, ne), _BF16)],
        compiler_params=_params("arbitrary"),
        name="cast_tables",
    )(peer_u, peer_v)


def _prep_layer(l, norm_mix_g, w_in, b_forget, q_norm_g, k_norm_g, v_norm_g, w_spatial, b_spatial,
                w_out_a, w_out_b, w_out, norm_ffn_g, w_peer_q, peer_q_norm_g, peer_sub_keys, peer_u, peer_v):
    fw = _HEADS * _HEAD_DIM
    w = w_in[l]
    o3 = 3 * fw
    o4 = o3 + _HEADS
    w_main = jnp.concatenate([w[:, :o3], w[:, o4:]], axis=1).astype(_BF16)
    wf = jnp.pad(w[:, o3:o4], ((0, 0), (0, _V7X_LANES - _HEADS))).astype(_BF16)
    bfp = jnp.pad(b_forget[l], (0, _V7X_LANES - _HEADS)).reshape(1, _V7X_LANES)
    tu, tvt = _cast_tables(l, peer_u, peer_v)
    return dict(
        g1=norm_mix_g[l].reshape(1, -1), w_main=w_main, wf=wf, bfp=bfp,
        qg=q_norm_g[l].reshape(1, -1), kg=k_norm_g[l].reshape(1, -1), vg=v_norm_g[l].reshape(1, -1),
        w_spatial=w_spatial[l], b_spatial=b_spatial[l],
        wa=w_out_a[l].astype(_BF16), wb=w_out_b[l].astype(_BF16), wo=w_out[l].astype(_BF16),
        g2=norm_ffn_g[l].reshape(1, -1), wq=w_peer_q[l].astype(_BF16), pg=peer_q_norm_g[l].reshape(1, -1),
        keys=peer_sub_keys[l].astype(_BF16), tu=tu, tvt=tvt,
    )


def _sgu_weights(p, frames):
    reps = _SGU_CHUNK // frames
    w = p["w_spatial"][:, :frames, :frames]
    eye = jnp.eye(reps, dtype=w.dtype)
    ws = jnp.einsum("ab,gij->gaibj", eye, w).reshape(_HEADS, _SGU_CHUNK, _SGU_CHUNK).astype(_BF16)
    bs = jnp.tile(p["b_spatial"][:, :frames], (1, reps))
    bs = jnp.broadcast_to(bs[:, :, None], (_HEADS, _SGU_CHUNK, _HEAD_DIM))
    return ws, bs


def _peer(p, x1, xn, xnt):
    s1, s2 = _peer_scores(xn, p["wq"], p["pg"], p["keys"])
    s1c, tau = _peer_topk(s1, s2)
    return _peer_dense(x1, xnt, s1c, s2, tau, p["tu"], p["tvt"])


def _layer_prompt(p, xp):
    b, s, d = xp.shape
    fw = _HEADS * _HEAD_DIM
    x = xp.reshape(b * s, d)
    ws, bs = _sgu_weights(p, _SGU_CHUNK)
    q, k, v, lf, ob, ga, gb, vt = _in_proj(x, p["g1"], p["w_main"], p["wf"], p["bfp"], p["qg"], p["kg"], p["vg"],
                                           ws, bs, prompt=True)
    aq, ak = _fcum_aug(lf.reshape(b, s, -1))
    oa = _fox_prompt(q.reshape(b, s, fw), aq, k.reshape(b, s, fw), ak, vt)
    x1, xn, xnt = _merge(x, oa.reshape(b * s, fw), ob, ga, gb, p["wa"], p["wb"], p["wo"], p["g2"])
    y = _peer(p, x1, xn, xnt)
    return (y.reshape(b, s, d), k.reshape(b, s, _HEADS, _HEAD_DIM), v.reshape(b, s, _HEADS, _HEAD_DIM),
            lf[:, :_HEADS].reshape(b, s, _HEADS))


def _layer_sample(l, p, xs, cache_k, cache_v, clf):
    b, t, d = xs.shape
    fw = _HEADS * _HEAD_DIM
    x = xs.reshape(b * t, d)
    ws, bs = _sgu_weights(p, t)
    q, k, v, lf, ob, ga, gb, sv = _in_proj(x, p["g1"], p["w_main"], p["wf"], p["bfp"], p["qg"], p["kg"], p["vg"],
                                           ws, bs, prompt=False)
    lf3 = lf.reshape(b, t, -1)
    lft = jnp.swapaxes(lf3[:, :, :_HEADS], 1, 2)
    clft = jnp.swapaxes(clf.astype(_F32), 1, 2)
    oa = _fox_decode(l, q.reshape(b, t, fw), k.reshape(b, t, fw), v.reshape(b, t, fw), lf3, lft,
                     cache_k, cache_v, clft)
    x1, xn, xnt = _merge(x, oa.reshape(b * t, fw), ob, ga, gb, p["wa"], p["wb"], p["wo"], p["g2"])
    y = _peer(p, x1, xn, xnt)
    return (y.reshape(b, t, d), k.reshape(b, t, _HEADS, _HEAD_DIM), v.reshape(b, t, _HEADS, _HEAD_DIM),
            lf[:, :_HEADS].reshape(b, t, _HEADS), sv.reshape(b, t, fw))


def kernel(x_prompt, x_sample, cache_k, cache_v, cache_logf, norm_mix_g, w_in, b_forget, q_norm_g, k_norm_g, v_norm_g, w_spatial, b_spatial, w_out_a, w_out_b, w_out, norm_ffn_g, w_peer_q, peer_q_norm_g, peer_sub_keys, peer_u, peer_v):
    xp, xs = x_prompt, x_sample
    kp_l, vp_l, fp_l, ks_l, vs_l, fs_l, us_l = [], [], [], [], [], [], []
    for l in range(w_in.shape[0]):
        p = _prep_layer(l, norm_mix_g, w_in, b_forget, q_norm_g, k_norm_g, v_norm_g, w_spatial, b_spatial,
                        w_out_a, w_out_b, w_out, norm_ffn_g, w_peer_q, peer_q_norm_g, peer_sub_keys, peer_u, peer_v)
        xp, k, v, f = _layer_prompt(p, xp)
        kp_l.append(k)
        vp_l.append(v)
        fp_l.append(f)
        xs, k, v, f, u = _layer_sample(l, p, xs, cache_k, cache_v, cache_logf[l])
        ks_l.append(k)
        vs_l.append(v)
        fs_l.append(f)
        us_l.append(u)
    return (xp, xs, jnp.stack(kp_l), jnp.stack(vp_l), jnp.stack(fp_l),
            jnp.stack(ks_l), jnp.stack(vs_l), jnp.stack(fs_l), jnp.stack(us_l))
```

```python
import functools

import jax
import jax.numpy as jnp
from jax import lax
from jax.experimental import pallas as pl
from jax.experimental.pallas import tpu as pltpu

_F32 = jnp.float32
_BF16 = jnp.bfloat16

_V7X_LANES = 128
_V7X_VMEM_BYTES = 64 * 1024 * 1024
_VMEM_LIMIT = _V7X_VMEM_BYTES - 8 * 1024 * 1024

_RMS_EPS = 1e-6
_NEG = -1e30
_HEADS = 8
_HEAD_DIM = 128
_FOX_SCALE = _HEAD_DIM ** -0.5
_LOG2E = 1.4426950408889634
_PEER_TOPK = 16
_SGU_CHUNK = 128


def _params(*sem):
    return pltpu.CompilerParams(dimension_semantics=sem, vmem_limit_bytes=_VMEM_LIMIT)


def _rms(x):
    return x * lax.rsqrt(jnp.mean(x * x, axis=-1, keepdims=True) + _RMS_EPS)


def _log_sigmoid(x):
    return jnp.minimum(x, 0.0) - jnp.log1p(jnp.exp(-jnp.abs(x)))


def _dot(a, b):
    return jnp.dot(a, b, preferred_element_type=_F32)


def _dot_nt(a, b):
    return lax.dot_general(a, b, (((1,), (1,)), ((), ())), preferred_element_type=_F32)


def _tile(n, pref):
    t = min(n, pref)
    while n % t:
        t //= 2
    return t


def _in_proj_body(x_ref, g_ref, w_ref, wf_ref, bf_ref, qg_ref, kg_ref, vg_ref, ws_ref, bs_ref,
                  q_ref, k_ref, v_ref, lf_ref, ob_ref, ga_ref, gb_ref, extra_ref, hn_scr, u_scr, *, prompt):
    j = pl.program_id(1)
    hd = _HEAD_DIM

    @pl.when(j == 0)
    def _():
        hn = (_rms(x_ref[...]) * g_ref[...]).astype(_BF16)
        hn_scr[...] = hn
        f = _dot(hn, wf_ref[...]) + bf_ref[...]
        lane = lax.broadcasted_iota(jnp.int32, f.shape, 1)
        lf_ref[...] = jnp.where(lane < _HEADS, _log_sigmoid(f), 0.0)

    def proj():
        return _dot(hn_scr[...], w_ref[...])

    @pl.when(j == 0)
    def _():
        z = proj()
        for h in range(_HEADS):
            sl = slice(h * hd, (h + 1) * hd)
            q_ref[:, sl] = (_rms(z[:, sl]) * (qg_ref[...] * (_FOX_SCALE * _LOG2E))).astype(_BF16)

    @pl.when(j == 1)
    def _():
        z = proj()
        for h in range(_HEADS):
            sl = slice(h * hd, (h + 1) * hd)
            k_ref[:, sl] = _rms(z[:, sl]) * kg_ref[...]

    @pl.when(j == 2)
    def _():
        z = proj()
        v_ref[...] = z
        if prompt:
            extra_ref[...] = z.T.astype(_BF16)

    @pl.when(j == 3)
    def _():
        u_scr[...] = jax.nn.gelu(proj())

    @pl.when(j == 4)
    def _():
        z = proj()
        vs = _rms(jax.nn.gelu(z)) * vg_ref[...]
        if not prompt:
            extra_ref[...] = vs
        c = _SGU_CHUNK
        row = lax.broadcasted_iota(jnp.int32, (c, c), 0)
        col = lax.broadcasted_iota(jnp.int32, (c, c), 1)
        for g in range(_HEADS):
            w = jnp.where(row >= col, ws_ref[g], jnp.zeros((), _BF16))
            for t in range(z.shape[0] // c):
                rs, cs = slice(t * c, (t + 1) * c), slice(g * hd, (g + 1) * hd)
                mixed = _dot(w, vs[rs, cs].astype(_BF16)) + bs_ref[g]
                ob_ref[rs, cs] = (u_scr[rs, cs] * mixed).astype(_BF16)

    @pl.when((j == 5) | (j == 6))
    def _():
        ga_ref[...] = jax.nn.sigmoid(proj()).astype(_BF16)

    @pl.when(j >= 7)
    def _():
        gb_ref[...] = jax.nn.sigmoid(proj()).astype(_BF16)


def _in_proj(x, g, w_main, wf, bfp, qg, kg, vg, ws, bs, *, prompt):
    n, d = x.shape
    fw = _HEADS * _HEAD_DIM
    tm = _tile(n, 512)
    nblk = w_main.shape[1] // fw
    const = lambda i, j: (0, 0)
    row = lambda i, j: (i, 0)
    out_shape = [
        jax.ShapeDtypeStruct((n, fw), _BF16),
        jax.ShapeDtypeStruct((n, fw), _F32),
        jax.ShapeDtypeStruct((n, fw), _F32),
        jax.ShapeDtypeStruct((n, _V7X_LANES), _F32),
        jax.ShapeDtypeStruct((n, fw), _BF16),
        jax.ShapeDtypeStruct((n, 2 * fw), _BF16),
        jax.ShapeDtypeStruct((n, 2 * fw), _BF16),
    ]
    out_specs = [
        pl.BlockSpec((tm, fw), row), pl.BlockSpec((tm, fw), row), pl.BlockSpec((tm, fw), row),
        pl.BlockSpec((tm, _V7X_LANES), row), pl.BlockSpec((tm, fw), row),
        pl.BlockSpec((tm, fw), lambda i, j: (i, jnp.clip(j - 5, 0, 1))),
        pl.BlockSpec((tm, fw), lambda i, j: (i, jnp.clip(j - 7, 0, 1))),
    ]
    if prompt:
        out_shape.append(jax.ShapeDtypeStruct((fw, n), _BF16))
        out_specs.append(pl.BlockSpec((fw, tm), lambda i, j: (0, i)))
    else:
        out_shape.append(jax.ShapeDtypeStruct((n, fw), _F32))
        out_specs.append(pl.BlockSpec((tm, fw), row))
    return pl.pallas_call(
        functools.partial(_in_proj_body, prompt=prompt),
        grid=(n // tm, nblk),
        in_specs=[
            pl.BlockSpec((tm, d), row),
            pl.BlockSpec((1, d), const),
            pl.BlockSpec((d, fw), lambda i, j: (0, j)),
            pl.BlockSpec((d, _V7X_LANES), const),
            pl.BlockSpec((1, _V7X_LANES), const),
            pl.BlockSpec((1, _HEAD_DIM), const),
            pl.BlockSpec((1, _HEAD_DIM), const),
            pl.BlockSpec((1, fw), const),
            pl.BlockSpec(ws.shape, lambda i, j: (0, 0, 0)),
            pl.BlockSpec(bs.shape, lambda i, j: (0, 0, 0)),
        ],
        out_specs=out_specs,
        out_shape=out_shape,
        scratch_shapes=[pltpu.VMEM((tm, d), _BF16), pltpu.VMEM((tm, fw), _F32)],
        compiler_params=_params("arbitrary", "arbitrary"),
        name="in_proj",
    )(x, g, w_main, wf, bfp, qg, kg, vg, ws, bs)


def _split3(x):
    hi = x.astype(_BF16)
    r1 = x - hi.astype(_F32)
    mid = r1.astype(_BF16)
    lo = (r1 - mid.astype(_F32)).astype(_BF16)
    return hi, mid, lo


def _fcum_aug_body(lf_ref, aq_ref, ak_ref, carry_scr):
    @pl.when(pl.program_id(1) == 0)
    def _():
        carry_scr[...] = jnp.zeros_like(carry_scr)

    lf = lf_ref[0]
    n = lf.shape[0]
    tri = (lax.broadcasted_iota(jnp.int32, (n, n), 0) >= lax.broadcasted_iota(jnp.int32, (n, n), 1)).astype(_F32)
    cs = jnp.dot(tri, lf, precision=lax.Precision.HIGHEST, preferred_element_type=_F32) + carry_scr[0:1, :]
    carry_scr[...] = jnp.broadcast_to(cs[n - 1:n, :], carry_scr.shape)
    pieces = _split3(cs * _LOG2E)

    fw = _HEADS * _HEAD_DIM
    src = lax.broadcasted_iota(jnp.int32, (_V7X_LANES, fw), 0)
    dst = lax.broadcasted_iota(jnp.int32, (_V7X_LANES, fw), 1)

    def place(p):
        return jnp.where(dst == src * _HEAD_DIM + p, 1.0, 0.0).astype(_BF16)

    fq = _dot(pieces[0], place(0)) + _dot(pieces[1], place(1)) + _dot(pieces[2], place(2))
    fk = _dot(pieces[0], place(3)) + _dot(pieces[1], place(4)) + _dot(pieces[2], place(5))
    lane = lax.broadcasted_iota(jnp.int32, (1, fw), 1) % _HEAD_DIM
    aq_ref[0] = (fq + jnp.where((lane >= 3) & (lane < 6), 1.0, 0.0)).astype(_BF16)
    ak_ref[0] = (jnp.where(lane < 3, 1.0, 0.0) - fk).astype(_BF16)


def _fcum_aug(lf):
    b, s, lanes = lf.shape
    fw = _HEADS * _HEAD_DIM
    tl = _tile(s, 512)
    blk = lambda i, c: (i, c, 0)
    return pl.pallas_call(
        _fcum_aug_body,
        grid=(b, s // tl),
        in_specs=[pl.BlockSpec((1, tl, lanes), blk)],
        out_specs=[pl.BlockSpec((1, tl, fw), blk), pl.BlockSpec((1, tl, fw), blk)],
        out_shape=[jax.ShapeDtypeStruct((b, s, fw), _BF16)] * 2,
        scratch_shapes=[pltpu.VMEM((8, lanes), _F32)],
        compiler_params=_params("arbitrary", "arbitrary"),
        name="fcum_aug",
    )(lf)


def _fox_prompt_body(q_ref, aq_ref, k_ref, ak_ref, vt_ref, o_ref, kc_scr, m_scr, l_scr, acc_scr, s_scr, *, tq):
    qi = pl.program_id(2)
    hd = _HEAD_DIM

    @pl.when(qi == 0)
    def _():
        kc_scr[:, :hd] = k_ref[0].astype(_BF16)
        kc_scr[:, hd:] = ak_ref[0]

    qt = jnp.concatenate([q_ref[0].astype(_F32).T, aq_ref[0].astype(_F32).T], axis=0).astype(_BF16)
    m_scr[...] = jnp.full_like(m_scr, _NEG)
    l_scr[...] = jnp.zeros_like(l_scr)
    acc_scr[...] = jnp.zeros_like(acc_scr)

    def scores(kj):
        return _dot(kc_scr[pl.ds(pl.multiple_of(kj * tq, tq), tq), :], qt)

    def absorb(kj, st, diag):
        if diag:
            key = lax.broadcasted_iota(jnp.int32, st.shape, 0)
            qry = lax.broadcasted_iota(jnp.int32, st.shape, 1)
            st = jnp.where(key <= qry, st, _NEG)
        m_old = m_scr[...]
        m_new = jnp.maximum(m_old, jnp.max(st, axis=0, keepdims=True))
        alpha = jnp.exp2(m_old - m_new)
        pt = jnp.exp2(st - m_new)
        l_scr[...] = alpha * l_scr[...] + jnp.sum(pt, axis=0, keepdims=True)
        acc_scr[...] = alpha * acc_scr[...] + _dot(vt_ref[:, pl.ds(pl.multiple_of(kj * tq, tq), tq)], pt.astype(_BF16))
        m_scr[...] = m_new

    def tile(kj, diag):
        absorb(kj, scores(kj), diag)

    group = s_scr.shape[0]

    def parked(first, n):
        for g in range(n):
            s_scr[g] = scores(first + g)
        for g in range(n):
            absorb(first + g, s_scr[g], False)

    def many(i, c):
        parked(group * i, group)
        return c

    full = qi // group
    lax.fori_loop(0, full, many, 0)
    rest = qi - full * group

    @pl.when(rest >= 4)
    def _():
        parked(full * group, 4)

    @pl.when(rest % 4 >= 2)
    def _():
        parked(qi - rest % 4, 2)

    @pl.when(rest % 2 == 1)
    def _():
        tile(qi - 1, False)

    tile(qi, True)
    o_ref[0] = (acc_scr[...] / l_scr[...]).T.astype(o_ref.dtype)


def _fox_prompt(q, aq, k, ak, vt):
    b, s, fw = q.shape
    hd = _HEAD_DIM
    tq = _tile(s, 512)
    qblk = lambda i, h, t: (i, t, h)
    kblk = lambda i, h, t: (i, 0, h)
    return pl.pallas_call(
        functools.partial(_fox_prompt_body, tq=tq),
        grid=(b, _HEADS, s // tq),
        in_specs=[
            pl.BlockSpec((1, tq, hd), qblk), pl.BlockSpec((1, tq, hd), qblk),
            pl.BlockSpec((1, s, hd), kblk), pl.BlockSpec((1, s, hd), kblk),
            pl.BlockSpec((hd, s), lambda i, h, t: (h, i)),
        ],
        out_specs=pl.BlockSpec((1, tq, hd), qblk),
        out_shape=jax.ShapeDtypeStruct((b, s, fw), _BF16),
        scratch_shapes=[
            pltpu.VMEM((s, 2 * hd), _BF16),
            pltpu.VMEM((1, tq), _F32), pltpu.VMEM((1, tq), _F32), pltpu.VMEM((hd, tq), _F32),
            pltpu.VMEM((8, tq, tq), _F32),
        ],
        compiler_params=_params("arbitrary", "arbitrary", "arbitrary"),
        name="fox_prompt",
    )(q, aq, k, ak, vt)


def _fox_decode_body(q_ref, kn_ref, vn_ref, lf_ref, lft_ref, ck_ref, cv_ref, clft_ref, o_ref,
                     carry_scr, m_scr, l_scr, acc_scr):
    kt = pl.program_id(1)
    nkt = pl.num_programs(1)
    hd = _HEAD_DIM
    t_new = q_ref.shape[1]
    tk = clft_ref.shape[2]

    lf = lf_ref[0]
    step = lax.broadcasted_iota(jnp.int32, lf.shape, 0)
    e_col = jnp.zeros_like(lf)
    for i in range(t_new):
        e_col = e_col + jnp.where(step >= i, lf[i:i + 1, :], 0.0)
    e_col = e_col * _LOG2E

    @pl.when(kt == 0)
    def _():
        carry_scr[...] = jnp.zeros_like(carry_scr)
        m_scr[...] = jnp.full_like(m_scr, _NEG)
        l_scr[...] = jnp.zeros_like(l_scr)
        acc_scr[...] = jnp.zeros_like(acc_scr)

    def update(h, s, vt):
        m_old = m_scr[h]
        m_new = jnp.maximum(m_old, jnp.max(s, axis=-1, keepdims=True))
        alpha = jnp.exp2(m_old - m_new)
        p = jnp.exp2(s - m_new)
        l_scr[h] = alpha * l_scr[h] + jnp.sum(p, axis=-1, keepdims=True)
        acc_scr[h] = alpha * acc_scr[h] + _dot(p.astype(_BF16), vt)
        m_scr[h] = m_new

    clft = clft_ref[0]
    after = (lax.broadcasted_iota(jnp.int32, (tk, tk), 0) > lax.broadcasted_iota(jnp.int32, (tk, tk), 1)).astype(_F32)
    d_row = jnp.dot(clft, after, precision=lax.Precision.HIGHEST, preferred_element_type=_F32) + carry_scr[:, 0:1]
    d_row = d_row * _LOG2E
    carry_scr[...] = carry_scr[...] + jnp.sum(clft, axis=-1, keepdims=True)
    for h in range(_HEADS):
        sl = slice(h * hd, (h + 1) * hd)
        rows = pl.ds(h, tk, stride=_HEADS)
        s = _dot_nt(q_ref[0, :, sl], ck_ref[0, rows, :].astype(_BF16)) + e_col[:, h:h + 1] + d_row[h:h + 1, :]
        update(h, s, cv_ref[0, rows, :].astype(_BF16))

    @pl.when(kt == nkt - 1)
    def _():
        lft = lft_ref[0]
        pos = lax.broadcasted_iota(jnp.int32, lft.shape, 1)
        e_row = jnp.zeros_like(lft)
        for i in range(t_new):
            e_row = e_row + jnp.where(pos >= i, lft[:, i:i + 1], 0.0)
        e_row = e_row * _LOG2E
        row = lax.broadcasted_iota(jnp.int32, (t_new, t_new), 0)
        col = lax.broadcasted_iota(jnp.int32, (t_new, t_new), 1)
        for h in range(_HEADS):
            sl = slice(h * hd, (h + 1) * hd)
            s = _dot_nt(q_ref[0, :, sl], kn_ref[0, :, sl].astype(_BF16)) + e_col[:, h:h + 1] - e_row[h:h + 1, :]
            update(h, jnp.where(col <= row, s, _NEG), vn_ref[0, :, sl].astype(_BF16))
            o_ref[0, :, sl] = (acc_scr[h] / l_scr[h]).astype(o_ref.dtype)


def _fox_decode(l, q, kn, vn, lf, lft, cache_k, cache_v, clft):
    b, t, fw = q.shape
    p = cache_k.shape[2]
    ck = cache_k.reshape(-1, p * _HEADS, _HEAD_DIM)
    cv = cache_v.reshape(-1, p * _HEADS, _HEAD_DIM)
    tk = _tile(p, 1024)
    nkt = p // tk
    new = lambda i, c: (i, 0, 0)
    cache = pl.BlockSpec((1, tk * _HEADS, _HEAD_DIM), lambda i, c: (l * b + i, nkt - 1 - c, 0))
    return pl.pallas_call(
        _fox_decode_body,
        grid=(b, nkt),
        in_specs=[
            pl.BlockSpec((1, t, fw), new), pl.BlockSpec((1, t, fw), new), pl.BlockSpec((1, t, fw), new),
            pl.BlockSpec((1, t, lf.shape[2]), new), pl.BlockSpec((1, _HEADS, t), new),
            cache, cache,
            pl.BlockSpec((1, _HEADS, tk), lambda i, c: (i, 0, nkt - 1 - c)),
        ],
        out_specs=pl.BlockSpec((1, t, fw), new),
        out_shape=jax.ShapeDtypeStruct((b, t, fw), _BF16),
        scratch_shapes=[
            pltpu.VMEM((_HEADS, _V7X_LANES), _F32),
            pltpu.VMEM((_HEADS, t, 1), _F32), pltpu.VMEM((_HEADS, t, 1), _F32),
            pltpu.VMEM((_HEADS, t, _HEAD_DIM), _F32),
        ],
        compiler_params=_params("arbitrary", "arbitrary"),
        name="fox_decode",
    )(q, kn, vn, lf, lft, ck, cv, clft)


def _merge_body(x_ref, oa_ref, ob_ref, ga_ref, gb_ref, wa_ref, wb_ref, wo_ref, g2_ref, x1_ref, xn_ref, xnt_ref):
    y = ga_ref[...].astype(_F32) * _dot(oa_ref[...], wa_ref[...]) + gb_ref[...].astype(_F32) * _dot(ob_ref[...], wb_ref[...])
    x1 = x_ref[...] + _dot(y.astype(_BF16), wo_ref[...])
    x1_ref[...] = x1
    xn = _rms(x1) * g2_ref[...]
    xn_ref[...] = xn.astype(_BF16)
    xnt_ref[...] = xn.T.astype(_BF16)


def _merge(x, oa, ob, ga, gb, wa, wb, wo, g2):
    n, d = x.shape
    fw = oa.shape[1]
    tm = _tile(n, 256)
    row = lambda i: (i, 0)
    const = lambda i: (0, 0)
    once = pl.Buffered(1)
    return pl.pallas_call(
        _merge_body,
        grid=(n // tm,),
        in_specs=[
            pl.BlockSpec((tm, d), row), pl.BlockSpec((tm, fw), row), pl.BlockSpec((tm, fw), row),
            pl.BlockSpec((tm, d), row), pl.BlockSpec((tm, d), row),
            pl.BlockSpec((fw, d), const, pipeline_mode=once), pl.BlockSpec((fw, d), const, pipeline_mode=once),
            pl.BlockSpec((d, d), const, pipeline_mode=once), pl.BlockSpec((1, d), const),
        ],
        out_specs=[pl.BlockSpec((tm, d), row), pl.BlockSpec((tm, d), row), pl.BlockSpec((d, tm), lambda i: (0, i))],
        out_shape=[jax.ShapeDtypeStruct((n, d), _F32), jax.ShapeDtypeStruct((n, d), _BF16),
                   jax.ShapeDtypeStruct((d, n), _BF16)],
        compiler_params=_params("arbitrary"),
        name="merge",
    )(x, oa, ob, ga, gb, wa, wb, wo, g2)


def _peer_scores_body(xn_ref, wq_ref, pg_ref, keys_ref, s1_ref, s2_ref):
    pq = _dot(xn_ref[...], wq_ref[...])
    kd = pg_ref.shape[1]
    half = kd // 2
    for h in range(_HEADS):
        qn = (_rms(pq[:, h * kd:(h + 1) * kd]) * pg_ref[...]).astype(_BF16)
        s1_ref[h] = _dot_nt(keys_ref[h, 0], qn[:, :half]) * _LOG2E
        s2_ref[h] = _dot_nt(keys_ref[h, 1], qn[:, half:]) * _LOG2E


def _peer_scores(xn, wq, pg, keys):
    n, d = xn.shape
    nk = keys.shape[2]
    tm = _tile(n, 512)
    return pl.pallas_call(
        _peer_scores_body,
        grid=(n // tm,),
        in_specs=[
            pl.BlockSpec((tm, d), lambda i: (i, 0)),
            pl.BlockSpec(wq.shape, lambda i: (0, 0), pipeline_mode=pl.Buffered(1)),
            pl.BlockSpec(pg.shape, lambda i: (0, 0)),
            pl.BlockSpec(keys.shape, lambda i: (0, 0, 0, 0)),
        ],
        out_specs=[pl.BlockSpec((_HEADS, nk, tm), lambda i: (0, 0, i))] * 2,
        out_shape=[jax.ShapeDtypeStruct((_HEADS, nk, n), _F32)] * 2,
        compiler_params=_params("arbitrary"),
        name="peer_scores",
    )(xn, wq, pg, keys)


def _top_rows(a, k):
    out = jnp.full((k, a.shape[1]), _NEG, _F32)
    slot = lax.broadcasted_iota(jnp.int32, out.shape, 0)
    for r in range(k):
        m = jnp.max(a, axis=0, keepdims=True)
        out = jnp.where(slot == r, m, out)
        a = jnp.where(a == m, _NEG, a)
    return out


def _pair_sums(t1, t2):
    half = _PEER_TOPK // 2
    parts = [t2 + t1[0:1, :]]
    parts += [t2[:half, :] + t1[a:a + 1, :] for a in range(1, half)]
    parts.append(t2[0:1, :] + t1[half:, :])
    return jnp.concatenate(parts, axis=0)


def _peer_topk_body(s1_ref, s2_ref, s1c_ref, tau_ref):
    k = _PEER_TOPK
    for h in range(_HEADS):
        s1 = s1_ref[h]
        t1 = _top_rows(s1, k)
        t2 = _top_rows(s2_ref[h], k)
        top = _top_rows(_pair_sums(t1, t2), k)
        z = jnp.sum(jnp.exp2(top - top[0:1, :]), axis=0, keepdims=True)
        cz = top[0:1, :] + jnp.log2(z)
        s1c_ref[h] = s1 - cz
        tau_ref[h:h + 1, :] = _top_rows(_pair_sums(t1 - cz, t2), k)[k - 1:k, :]


def _peer_topk(s1, s2):
    heads, nk, n = s1.shape
    tl = _tile(n, 256)
    sblk = pl.BlockSpec((heads, nk, tl), lambda i: (0, 0, i))
    oblk = pl.BlockSpec((heads, tl), lambda i: (0, i))
    return pl.pallas_call(
        _peer_topk_body,
        grid=(n // tl,),
        in_specs=[sblk, sblk],
        out_specs=[sblk, oblk],
        out_shape=[jax.ShapeDtypeStruct((heads, nk, n), _F32), jax.ShapeDtypeStruct((heads, n), _F32)],
        compiler_params=_params("arbitrary"),
        name="peer_topk",
    )(s1, s2)


def _peer_dense_body(x1_ref, xnt_ref, s1a_ref, s1b_ref, s2_ref, tau_ref, u_ref, vt_ref, o_ref,
                     acc_scr, w_scr, wa_scr, s1x_scr, *, pieces, tl):
    j = pl.program_id(1)
    _, te, tm = w_scr.shape
    nk = s2_ref.shape[1]
    groups = te // nk
    d = acc_scr.shape[0]
    cur = j % 2
    nxt = 1 - cur

    def expand(s1_ref):
        for h in range(_HEADS):
            for r in range(groups):
                s1x_scr[h, r] = jnp.broadcast_to(s1_ref[h, r:r + 1, :], (8, tm))

    def gate_rows(r, slot):
        for lc in range(tm // tl):
            ls = slice(lc * tl, (lc + 1) * tl)
            w = None
            for h in range(_HEADS):
                y = s2_ref[h, :, ls] + s1x_scr[h, r, 0:1, ls]
                g = jnp.where(y >= tau_ref[h:h + 1, ls], jnp.exp2(y), 0.0)
                w = g if w is None else w + g
            w_scr[slot, pl.ds(pl.multiple_of(r * nk, nk), nk), ls] = w

    last = pl.num_programs(1) - 1

    @pl.when(j == 0)
    def _():
        acc_scr[...] = jnp.zeros_like(acc_scr)
        expand(s1a_ref)

        def first(r, c):
            gate_rows(r, 0)
            return c

        lax.fori_loop(0, groups, first, 0)

    expand(s1b_ref)
    ru, rd, gr = te // pieces, d // pieces, groups // pieces

    def consume(p):
        rows_d = pl.ds(pl.multiple_of(p * rd, rd), rd)
        acc_scr[rows_d, :] += _dot(vt_ref[rows_d, :], wa_scr[nxt])

    def produce(p):
        rows_u = pl.ds(pl.multiple_of(p * ru, ru), ru)
        a = jax.nn.gelu(_dot(u_ref[rows_u, :], xnt_ref[...]))
        wa_scr[cur, rows_u, :] = (w_scr[cur, rows_u, :] * a).astype(_BF16)
        for rr in range(gr):
            gate_rows(p * gr + rr, nxt)

    def both(p, c):
        consume(p)
        produce(p)
        return c

    def only_produce(p, c):
        produce(p)
        return c

    def only_consume(p, c):
        consume(p)
        return c

    @pl.when(j == 0)
    def _():
        lax.fori_loop(0, pieces, only_produce, 0)

    @pl.when((j > 0) & (j < last))
    def _():
        lax.fori_loop(0, pieces, both, 0)

    @pl.when(j == last)
    def _():
        lax.fori_loop(0, pieces, only_consume, 0)

    @pl.when(j == last)
    def _():
        o_ref[...] = x1_ref[...] + acc_scr[...].T


def _peer_dense(x1, xnt, s1, s2, tau, tu, tvt):
    n, d = x1.shape
    heads, nk, _ = s1.shape
    ne = tu.shape[0]
    tm = _tile(n, 512)
    groups = 8
    te = groups * nk
    tok = lambda t, j: (t, 0)
    once = pl.Buffered(1)
    nj = ne // te
    s1_first = pl.BlockSpec((heads, groups, tm), lambda t, j: (0, 0, t))
    s1_next = pl.BlockSpec((heads, groups, tm), lambda t, j: (0, jnp.minimum(j + 1, nj - 1), t))
    return pl.pallas_call(
        functools.partial(_peer_dense_body, pieces=2, tl=_V7X_LANES),
        grid=(n // tm, nj + 1),
        in_specs=[
            pl.BlockSpec((tm, d), tok, pipeline_mode=once),
            pl.BlockSpec((d, tm), lambda t, j: (0, t), pipeline_mode=once),
            s1_first, s1_next,
            pl.BlockSpec((heads, nk, tm), lambda t, j: (0, 0, t), pipeline_mode=once),
            pl.BlockSpec((heads, tm), lambda t, j: (0, t)),
            pl.BlockSpec((te, d), lambda t, j: (jnp.minimum(j, nj - 1), 0)),
            pl.BlockSpec((d, te), lambda t, j: (0, jnp.maximum(j - 1, 0))),
        ],
        out_specs=pl.BlockSpec((tm, d), tok),
        out_shape=jax.ShapeDtypeStruct((n, d), _F32),
        scratch_shapes=[
            pltpu.VMEM((d, tm), _F32), pltpu.VMEM((2, te, tm), _F32), pltpu.VMEM((2, te, tm), _BF16),
            pltpu.VMEM((heads, groups, 8, tm), _F32),
        ],
        compiler_params=_params("arbitrary", "arbitrary"),
        name="peer_dense",
    )(x1, xnt, s1, s1, s2, tau, tu, tvt)


def _cast_tables_body(u_ref, v_ref, ub_ref, vtb_ref):
    ub_ref[...] = u_ref[0].astype(_BF16)
    vtb_ref[...] = v_ref[0].T.astype(_BF16)


def _cast_tables(l, peer_u, peer_v):
    _, ne, d = peer_u.shape
    tr = _tile(ne, 512)
    src = pl.BlockSpec((1, tr, d), lambda i: (l, i, 0))
    return pl.pallas_call(
        _cast_tables_body,
        grid=(ne // tr,),
        in_specs=[src, src],
        out_specs=[pl.BlockSpec((tr, d), lambda i: (i, 0)), pl.BlockSpec((d, tr), lambda i: (0, i))],
        out_shape=[jax.ShapeDtypeStruct((ne, d), _BF16), jax.ShapeDtypeStruct((d, ne), _BF16)],
        compiler_params=_params("arbitrary"),
        name="cast_tables",
    )(peer_u, peer_v)


def _prep_layer(l, norm_mix_g, w_in, b_forget, q_norm_g, k_norm_g, v_norm_g, w_spatial, b_spatial,
                w_out_a, w_out_b, w_out, norm_ffn_g, w_peer_q, peer_q_norm_g, peer_sub_keys, peer_u, peer_v):
    fw = _HEADS * _HEAD_DIM
    w = w_in[l]
    o3 = 3 * fw
    o4 = o3 + _HEADS
    w_main = jnp.concatenate([w[:, :o3], w[:, o4:]], axis=1).astype(_BF16)
    wf = jnp.pad(w[:, o3:o4], ((0, 0), (0, _V7X_LANES - _HEADS))).astype(_BF16)
    bfp = jnp.pad(b_forget[l], (0, _V7X_LANES - _HEADS)).reshape(1, _V7X_LANES)
    tu, tvt = _cast_tables(l, peer_u, peer_v)
    return dict(
        g1=norm_mix_g[l].reshape(1, -1), w_main=w_main, wf=wf, bfp=bfp,
        qg=q_norm_g[l].reshape(1, -1), kg=k_norm_g[l].reshape(1, -1), vg=v_norm_g[l].reshape(1, -1),
        w_spatial=w_spatial[l], b_spatial=b_spatial[l],
        wa=w_out_a[l].astype(_BF16), wb=w_out_b[l].astype(_BF16), wo=w_out[l].astype(_BF16),
        g2=norm_ffn_g[l].reshape(1, -1), wq=w_peer_q[l].astype(_BF16), pg=peer_q_norm_g[l].reshape(1, -1),
        keys=peer_sub_keys[l].astype(_BF16), tu=tu, tvt=tvt,
    )


def _sgu_weights(p, frames):
    reps = _SGU_CHUNK // frames
    w = p["w_spatial"][:, :frames, :frames]
    eye = jnp.eye(reps, dtype=w.dtype)
    ws = jnp.einsum("ab,gij->gaibj", eye, w).reshape(_HEADS, _SGU_CHUNK, _SGU_CHUNK).astype(_BF16)
    bs = jnp.tile(p["b_spatial"][:, :frames], (1, reps))
    bs = jnp.broadcast_to(bs[:, :, None], (_HEADS, _SGU_CHUNK, _HEAD_DIM))
    return ws, bs


def _peer(p, x1, xn, xnt):
    s1, s2 = _peer_scores(xn, p["wq"], p["pg"], p["keys"])
    s1c, tau = _peer_topk(s1, s2)
    return _peer_dense(x1, xnt, s1c, s2, tau, p["tu"], p["tvt"])


def _layer_prompt(p, xp):
    b, s, d = xp.shape
    fw = _HEADS * _HEAD_DIM
    x = xp.reshape(b * s, d)
    ws, bs = _sgu_weights(p, _SGU_CHUNK)
    q, k, v, lf, ob, ga, gb, vt = _in_proj(x, p["g1"], p["w_main"], p["wf"], p["bfp"], p["qg"], p["kg"], p["vg"],
                                           ws, bs, prompt=True)
    aq, ak = _fcum_aug(lf.reshape(b, s, -1))
    oa = _fox_prompt(q.reshape(b, s, fw), aq, k.reshape(b, s, fw), ak, vt)
    x1, xn, xnt = _merge(x, oa.reshape(b * s, fw), ob, ga, gb, p["wa"], p["wb"], p["wo"], p["g2"])
    y = _peer(p, x1, xn, xnt)
    return (y.reshape(b, s, d), k.reshape(b, s, _HEADS, _HEAD_DIM), v.reshape(b, s, _HEADS, _HEAD_DIM),
            lf[:, :_HEADS].reshape(b, s, _HEADS))


def _layer_sample(l, p, xs, cache_k, cache_v, clf):
    b, t, d = xs.shape
    fw = _HEADS * _HEAD_DIM
    x = xs.reshape(b * t, d)
    ws, bs = _sgu_weights(p, t)
    q, k, v, lf, ob, ga, gb, sv = _in_proj(x, p["g1"], p["w_main"], p["wf"], p["bfp"], p["qg"], p["kg"], p["vg"],
                                           ws, bs, prompt=False)
    lf3 = lf.reshape(b, t, -1)
    lft = jnp.swapaxes(lf3[:, :, :_HEADS], 1, 2)
    clft = jnp.swapaxes(clf.astype(_F32), 1, 2)
    oa = _fox_decode(l, q.reshape(b, t, fw), k.reshape(b, t, fw), v.reshape(b, t, fw), lf3, lft,
                     cache_k, cache_v, clft)
    x1, xn, xnt = _merge(x, oa.reshape(b * t, fw), ob, ga, gb, p["wa"], p["wb"], p["wo"], p["g2"])
    y = _peer(p, x1, xn, xnt)
    return (y.reshape(b, t, d), k.reshape(b, t, _HEADS, _HEAD_DIM), v.reshape(b, t, _HEADS, _HEAD_DIM),
            lf[:, :_HEADS].reshape(b, t, _HEADS), sv.reshape(b, t, fw))


def kernel(x_prompt, x_sample, cache_k, cache_v, cache_logf, norm_mix_g, w_in, b_forget, q_norm_g, k_norm_g, v_norm_g, w_spatial, b_spatial, w_out_a, w_out_b, w_out, norm_ffn_g, w_peer_q, peer_q_norm_g, peer_sub_keys, peer_u, peer_v):
    xp, xs = x_prompt, x_sample
    kp_l, vp_l, fp_l, ks_l, vs_l, fs_l, us_l = [], [], [], [], [], [], []
    for l in range(w_in.shape[0]):
        p = _prep_layer(l, norm_mix_g, w_in, b_forget, q_norm_g, k_norm_g, v_norm_g, w_spatial, b_spatial,
                        w_out_a, w_out_b, w_out, norm_ffn_g, w_peer_q, peer_q_norm_g, peer_sub_keys, peer_u, peer_v)
        xp, k, v, f = _layer_prompt(p, xp)
        kp_l.append(k)
        vp_l.append(v)
        fp_l.append(f)
        xs, k, v, f, u = _layer_sample(l, p, xs, cache_k, cache_v, cache_logf[l])
        ks_l.append(k)
        vs_l.append(v)
        fs_l.append(f)
        us_l.append(u)
    return (xp, xs, jnp.stack(kp_l), jnp.stack(vp_l), jnp.stack(fp_l),
            jnp.stack(ks_l), jnp.stack(vs_l), jnp.stack(fs_l), jnp.stack(us_l))
```
